```python
import math
import jax
import jax.numpy as jnp
from jax import lax
import numpy as np

D_MODEL = 1024
BATCH = 4
SEQ = 4096
DEPTH = 4
DEC_BATCH = 128
DEC_SEQ = 4
PAST_LEN = 2048
PAGE_SIZE = 128

HEAD_DIM = 64
BRANCH_WIDTH = D_MODEL // 2
N_BRANCH = 3
H_MOBA = BRANCH_WIDTH // HEAD_DIM
MOBA_BLOCK = 256
MOBA_TOPK = 3
MOBA_QUERY_BUDGET = 128
H_RET = 4
DV_RET = BRANCH_WIDTH // H_RET
DK_RET = DV_RET // 2
RET_CHUNK = 128
H_SB = BRANCH_WIDTH // HEAD_DIM
SB_QBLOCK = 128
D_FF = 2816
D_PLE = 256
ROPE_BASE = 10000.0
RMS_EPS = 1e-6
IN_SIZES = (H_MOBA * HEAD_DIM,) * 3 + (H_RET * DK_RET,) * 2 + (H_RET * DV_RET,) * 2 + (H_SB * HEAD_DIM,) * 3
IN_WIDTH = sum(IN_SIZES)

kernel_name = 'hybrid_moba_retnet_stickbreak_decode_step'


def rmsnorm(x, g):
    xf = x.astype(jnp.float32)
    y = xf * lax.rsqrt(jnp.mean(xf * xf, axis=-1, keepdims=True) + RMS_EPS)
    return (y * g.astype(jnp.float32)).astype(x.dtype)


def head_rmsnorm(x):
    xf = x.astype(jnp.float32)
    return xf * lax.rsqrt(jnp.mean(xf * xf, axis=-1, keepdims=True) + RMS_EPS)


def swiglu_ffn(x, w_gu, w_down):
    g, u = jnp.split(x @ w_gu, 2, axis=-1)
    return (jax.nn.silu(g) * u) @ w_down


def rotary(x, pos):
    d = x.shape[-1]
    inv = ROPE_BASE ** (-jnp.arange(0, d, 2, dtype=jnp.float32) / d)
    ang = pos.astype(jnp.float32)[:, None] * inv[None, :]
    cos = jnp.cos(ang)[None, :, None, :]
    sin = jnp.sin(ang)[None, :, None, :]
    xf = x.astype(jnp.float32)
    x1, x2 = xf[..., : d // 2], xf[..., d // 2:]
    return jnp.concatenate([x1 * cos - x2 * sin, x1 * sin + x2 * cos], axis=-1).astype(x.dtype)


def moba_attend(q, k, v, q_pos):
    B, T, H, hd = q.shape
    L = k.shape[1]
    nb = -(-L // MOBA_BLOCK)
    pad = nb * MOBA_BLOCK - L
    kp = jnp.pad(k, ((0, 0), (0, pad), (0, 0), (0, 0)))
    vp = jnp.pad(v, ((0, 0), (0, pad), (0, 0), (0, 0)))
    kb = kp.reshape(B, nb, MOBA_BLOCK, H, hd).transpose(0, 3, 1, 2, 4)
    vb = vp.reshape(B, nb, MOBA_BLOCK, H, hd).transpose(0, 3, 1, 2, 4)
    kmean = jnp.mean(kb.astype(jnp.float32), axis=3)
    qc = math.gcd(T, max(1, MOBA_QUERY_BUDGET // B))
    n_chunks = T // qc
    q_chunks = q.transpose(0, 2, 1, 3).reshape(B, H, n_chunks, qc, hd).transpose(2, 0, 1, 3, 4)
    pos_chunks = q_pos.reshape(n_chunks, qc)
    n_sel = min(MOBA_TOPK, nb)
    scale = hd ** -0.5
    neg = jnp.finfo(jnp.float32).min
    bidx = jnp.arange(B)[:, None, None, None]
    hidx = jnp.arange(H)[None, :, None, None]
    blk_ids = jnp.arange(nb)
    offs = jnp.arange(MOBA_BLOCK)
    is_own = jnp.arange(n_sel + 1) == n_sel

    def one_chunk(args):
        qq, pp = args
        own = pp // MOBA_BLOCK
        gate = jnp.einsum('bhtd,bhnd->bhtn', qq.astype(jnp.float32), kmean)
        fully_past = blk_ids[None, :] < own[:, None]
        gate = jnp.where(fully_past[None, None], gate, neg)
        _, sel = lax.top_k(gate, n_sel)
        idx = jnp.concatenate(
            [sel.astype(jnp.int32), jnp.broadcast_to(own[None, None, :, None], (B, H, qc, 1)).astype(jnp.int32)], axis=-1)
        kg = kb[bidx, hidx, idx]
        vg = vb[bidx, hidx, idx]
        logits = jnp.einsum('bhtd,bhtjkd->bhtjk', qq, kg).astype(jnp.float32) * scale
        key_pos = idx[..., None] * MOBA_BLOCK + offs
        valid_sel = (idx < own[None, None, :, None])[..., None]
        valid_own = key_pos <= pp[None, None, :, None, None]
        valid = jnp.where(is_own[:, None], valid_own, valid_sel)
        logits = jnp.where(valid, logits, -jnp.inf)
        w = jax.nn.softmax(logits.reshape(B, H, qc, -1), axis=-1).reshape(logits.shape)
        return jnp.einsum('bhtjk,bhtjkd->bhtd', w.astype(v.dtype), vg)

    out = lax.map(one_chunk, (q_chunks, pos_chunks))
    return out.transpose(1, 0, 3, 2, 4).reshape(B, T, H, hd)


def stick_breaking_attend(q, k, v, q_pos):
    B, T, H, hd = q.shape
    L = k.shape[1]
    qb = math.gcd(T, SB_QBLOCK)
    n_blk = T // qb
    k_pos = jnp.arange(L)
    scale = hd ** -0.5
    q_blocks = q.reshape(B, n_blk, qb, H, hd).transpose(1, 0, 3, 2, 4)
    pos_blocks = q_pos.reshape(n_blk, qb)

    def one_block(args):
        qq, pp = args
        z = jnp.einsum('bhtd,bshd->bhts', qq, k).astype(jnp.float32) * scale
        causal = k_pos[None, :] < pp[:, None]
        log_keep = jnp.where(causal, jax.nn.log_sigmoid(-z), 0.0)
        later = lax.cumsum(log_keep, axis=3, reverse=True) - log_keep
        a = jnp.where(causal, jnp.exp(jax.nn.log_sigmoid(z) + later), 0.0)
        return jnp.einsum('bhts,bshd->bhtd', a.astype(v.dtype), v)

    out = lax.map(one_block, (q_blocks, pos_blocks))
    return out.transpose(1, 0, 3, 2, 4).reshape(B, T, H, hd)


def retention(q, k, v, s0):
    B, T, H, dk = q.shape
    dv = v.shape[-1]
    c = math.gcd(T, RET_CHUNK)
    n = T // c
    log_g = jnp.log(1.0 - 2.0 ** (-5.0 - jnp.arange(H, dtype=jnp.float32)))
    i = jnp.arange(c, dtype=jnp.float32)
    diff = i[:, None] - i[None, :]
    decay_mask = jnp.where(diff >= 0, jnp.exp(log_g[:, None, None] * jnp.maximum(diff, 0.0)), 0.0)
    q_decay = jnp.exp(log_g[:, None] * (i[None, :] + 1.0))[None, :, :, None]
    k_decay = jnp.exp(log_g[:, None] * (c - 1.0 - i[None, :]))[None, :, :, None]
    chunk_decay = jnp.exp(log_g * c)[None, :, None, None]
    to_chunks = lambda t: t.astype(jnp.float32).reshape(B, n, c, H, t.shape[-1]).transpose(1, 0, 3, 2, 4)

    def step(s, inp):
        qq, kk, vv = inp
        inner = jnp.einsum('bhid,bhjd->bhij', qq, kk) * decay_mask
        o = jnp.einsum('bhij,bhjv->bhiv', inner, vv) + jnp.einsum('bhid,bhdv->bhiv', qq, s) * q_decay
        s_new = s * chunk_decay + jnp.einsum('bhjd,bhjv->bhdv', kk * k_decay, vv)
        return s_new, o

    s_final, o = lax.scan(step, s0.astype(jnp.float32), (to_chunks(q), to_chunks(k), to_chunks(v)))
    return o.transpose(1, 0, 3, 2, 4).reshape(B, T, H, dv), s_final


def token_mixer(u, pos, past, ret_s0, w_in, w_gate, b_gate, w_branch, w_out):
    B, T, _ = u.shape
    past_mk, past_mv, past_sk, past_sv = past
    splits = np.cumsum(IN_SIZES)[:-1].tolist()
    mq, mk, mv, rq, rk, rv, rg, sq, sk, sv = jnp.split(u @ w_in, splits, axis=-1)
    heads = lambda t, h, d: t.reshape(B, T, h, d)
    mq, mk, mv = heads(mq, H_MOBA, HEAD_DIM), heads(mk, H_MOBA, HEAD_DIM), heads(mv, H_MOBA, HEAD_DIM)
    y_moba = moba_attend(mq, jnp.concatenate([past_mk, mk], axis=1), jnp.concatenate([past_mv, mv], axis=1), pos)
    rq = rotary(heads(rq, H_RET, DK_RET), pos)
    rk = rotary(heads(rk, H_RET, DK_RET), pos) * (DK_RET ** -0.5)
    o_ret, s_new = retention(rq, rk, heads(rv, H_RET, DV_RET), ret_s0)
    y_ret = head_rmsnorm(o_ret).astype(u.dtype).reshape(B, T, BRANCH_WIDTH) * jax.nn.silu(rg)
    sq, sk, sv = heads(sq, H_SB, HEAD_DIM), heads(sk, H_SB, HEAD_DIM), heads(sv, H_SB, HEAD_DIM)
    y_sb = stick_breaking_attend(sq, jnp.concatenate([past_sk, sk], axis=1), jnp.concatenate([past_sv, sv], axis=1), pos)
    branches = jnp.stack([y_moba.reshape(B, T, BRANCH_WIDTH), y_ret, y_sb.reshape(B, T, BRANCH_WIDTH)], axis=2)
    y_b = jnp.einsum('btnc,ncd->btnd', branches, w_branch)
    gates = jax.nn.sigmoid(u @ w_gate + b_gate).reshape(B, T, N_BRANCH, D_MODEL)
    merged = jnp.sum(gates * y_b, axis=2)
    return merged @ w_out, (mk, mv, sk, sv, s_new.astype(ret_s0.dtype))


def decoder_layer(x, p, pos, past, ret_s0, lw):
    (ffa_n, ffa_gu, ffa_dn, mix_n, w_in, w_gate, b_gate, w_branch, w_out,
     ffb_n, ffb_gu, ffb_dn, ple_n, ple_gate, ple_proj) = lw
    h = x + 0.5 * swiglu_ffn(rmsnorm(x, ffa_n), ffa_gu, ffa_dn)
    mix, new_state = token_mixer(rmsnorm(h, mix_n), pos, past, ret_s0, w_in, w_gate, b_gate, w_branch, w_out)
    h = h + mix
    h = h + 0.5 * swiglu_ffn(rmsnorm(h, ffb_n), ffb_gu, ffb_dn)
    h = h + jax.nn.sigmoid(rmsnorm(h, ple_n) @ ple_gate) * (p @ ple_proj)
    return h, new_state


def paged_rows(pool, page_table):
    rows = pool[page_table]
    b, npg, ps, h, d = rows.shape
    return rows.reshape(b, npg * ps, h, d)


def setup_inputs(seed: int = 0) -> dict:
    key = jax.random.key(seed)
    ks = jax.random.split(key, 32)
    f32 = jnp.float32
    nrm = lambda k, shape, s=1.0: jax.random.normal(k, shape, f32) * s
    gain = lambda k, shape: 1.0 + 0.05 * jax.random.normal(k, shape, f32)
    n_pages = PAST_LEN // PAGE_SIZE
    n_used = DEC_BATCH * n_pages
    n_pool = n_used + max(1, n_used // 4)
    perm = jax.random.permutation(ks[0], n_pool)
    page_table = perm[:n_used].reshape(DEC_BATCH, n_pages).astype(jnp.int32)
    return {
        'x_prompt': nrm(ks[1], (BATCH, SEQ, D_MODEL)),
        'x_sample': nrm(ks[2], (DEC_BATCH, DEC_SEQ, D_MODEL)),
        'cache_moba_k': nrm(ks[3], (DEPTH, n_pool, PAGE_SIZE, H_MOBA, HEAD_DIM)),
        'cache_moba_v': nrm(ks[4], (DEPTH, n_pool, PAGE_SIZE, H_MOBA, HEAD_DIM)),
        'cache_sb_k': nrm(ks[5], (DEPTH, n_pool, PAGE_SIZE, H_SB, HEAD_DIM)),
        'cache_sb_v': nrm(ks[6], (DEPTH, n_pool, PAGE_SIZE, H_SB, HEAD_DIM)),
        'state_ret': nrm(ks[7], (DEPTH, DEC_BATCH, H_RET, DK_RET, DV_RET)),
        'page_table': page_table,
        'p_prompt': nrm(ks[8], (DEPTH, BATCH, SEQ, D_PLE)),
        'p_sample': nrm(ks[9], (DEPTH, DEC_BATCH, DEC_SEQ, D_PLE)),
        'ffa_norm': gain(ks[10], (DEPTH, D_MODEL)),
        'ffa_w_gu': nrm(ks[11], (DEPTH, D_MODEL, 2 * D_FF), D_MODEL ** -0.5),
        'ffa_w_down': nrm(ks[12], (DEPTH, D_FF, D_MODEL), D_FF ** -0.5),
        'mix_norm': gain(ks[13], (DEPTH, D_MODEL)),
        'w_in': nrm(ks[14], (DEPTH, D_MODEL, IN_WIDTH), D_MODEL ** -0.5),
        'w_gate': nrm(ks[15], (DEPTH, D_MODEL, N_BRANCH * D_MODEL), D_MODEL ** -0.5),
        'b_gate': nrm(ks[16], (DEPTH, N_BRANCH * D_MODEL), 0.02),
        'w_branch': nrm(ks[17], (DEPTH, N_BRANCH, BRANCH_WIDTH, D_MODEL), BRANCH_WIDTH ** -0.5),
        'w_out': nrm(ks[18], (DEPTH, D_MODEL, D_MODEL), D_MODEL ** -0.5),
        'ffb_norm': gain(ks[19], (DEPTH, D_MODEL)),
        'ffb_w_gu': nrm(ks[20], (DEPTH, D_MODEL, 2 * D_FF), D_MODEL ** -0.5),
        'ffb_w_down': nrm(ks[21], (DEPTH, D_FF, D_MODEL), D_FF ** -0.5),
        'ple_norm': gain(ks[22], (DEPTH, D_MODEL)),
        'ple_w_gate': nrm(ks[23], (DEPTH, D_MODEL, D_MODEL), D_MODEL ** -0.5),
        'ple_w_proj': nrm(ks[24], (DEPTH, D_PLE, D_MODEL), D_PLE ** -0.5),
        'final_norm': gain(ks[25], (D_MODEL,)),
    }


def reference(x_prompt, x_sample, cache_moba_k, cache_moba_v, cache_sb_k, cache_sb_v, state_ret, page_table,
              p_prompt, p_sample, ffa_norm, ffa_w_gu, ffa_w_down, mix_norm, w_in, w_gate, b_gate, w_branch,
              w_out, ffb_norm, ffb_w_gu, ffb_w_down, ple_norm, ple_w_gate, ple_w_proj, final_norm):
    bp, tp, _ = x_prompt.shape
    bs, ts, _ = x_sample.shape
    past_len = page_table.shape[1] * cache_moba_k.shape[2]
    pos_p = jnp.arange(tp, dtype=jnp.int32)
    pos_s = past_len + jnp.arange(ts, dtype=jnp.int32)
    empty_m = jnp.zeros((bp, 0, H_MOBA, HEAD_DIM), x_prompt.dtype)
    empty_s = jnp.zeros((bp, 0, H_SB, HEAD_DIM), x_prompt.dtype)
    s0_p = jnp.zeros((bp, H_RET, DK_RET, DV_RET), jnp.float32)
    hp, hs = x_prompt, x_sample
    new_p = [[], [], [], [], []]
    new_s = [[], [], [], [], []]
    for i in range(DEPTH):
        lw = (ffa_norm[i], ffa_w_gu[i], ffa_w_down[i], mix_norm[i], w_in[i], w_gate[i], b_gate[i], w_branch[i],
              w_out[i], ffb_norm[i], ffb_w_gu[i], ffb_w_down[i], ple_norm[i], ple_w_gate[i], ple_w_proj[i])
        hp, st_p = decoder_layer(hp, p_prompt[i], pos_p, (empty_m, empty_m, empty_s, empty_s), s0_p, lw)
        past_s = (paged_rows(cache_moba_k[i], page_table), paged_rows(cache_moba_v[i], page_table),
                  paged_rows(cache_sb_k[i], page_table), paged_rows(cache_sb_v[i], page_table))
        hs, st_s = decoder_layer(hs, p_sample[i], pos_s, past_s, state_ret[i], lw)
        for lst, a in zip(new_p, st_p):
            lst.append(a)
        for lst, a in zip(new_s, st_s):
            lst.append(a)
    y_prompt = rmsnorm(hp, final_norm)
    y_sample = rmsnorm(hs, final_norm)
    moba_k_p, moba_v_p, sb_k_p, sb_v_p, ret_p = [jnp.stack(l) for l in new_p]
    moba_k_s, moba_v_s, sb_k_s, sb_v_s, ret_s = [jnp.stack(l) for l in new_s]
    return (y_prompt, y_sample, moba_k_p, moba_v_p, sb_k_p, sb_v_p, ret_p, moba_k_s, moba_v_s, sb_k_s, sb_v_s, ret_s)
```

```python
import functools
import math

import jax
import jax.numpy as jnp
from jax import lax
from jax.experimental import pallas as pl
from jax.experimental.pallas import tpu as pltpu

D_MODEL = 1024
HEAD_DIM = 64
BRANCH_WIDTH = D_MODEL // 2
N_BRANCH = 3
H_MOBA = BRANCH_WIDTH // HEAD_DIM
MOBA_BLOCK = 256
MOBA_TOPK = 3
H_RET = 4
DV_RET = BRANCH_WIDTH // H_RET
DK_RET = DV_RET // 2
RET_CHUNK = 128
H_SB = BRANCH_WIDTH // HEAD_DIM
ROPE_BASE = 10000.0
RMS_EPS = 1e-6

LANES = 128
HEADS_PER_LANE_TILE = LANES // HEAD_DIM
VMEM_LIMIT_BYTES = 56 * 1024 * 1024

_IN_SIZES = (H_MOBA * HEAD_DIM,) * 3 + (H_RET * DK_RET,) * 2 + (H_RET * DV_RET,) * 2 + (H_SB * HEAD_DIM,) * 3
_IN_OFFS = tuple(int(sum(_IN_SIZES[:i])) for i in range(len(_IN_SIZES) + 1))
_KV_COLS = (1, 2, 8, 9)

_F32 = jnp.float32
_BF16 = jnp.bfloat16
_SCALE = HEAD_DIM ** -0.5
_NEG_INF = float("-inf")


def _params(*sem):
    return pltpu.CompilerParams(dimension_semantics=sem, vmem_limit_bytes=VMEM_LIMIT_BYTES)


def _resident(shape, index_map):
    return pl.BlockSpec(shape, index_map, pipeline_mode=pl.Buffered(1))


def _dot(a, b):
    return jnp.dot(a, b, preferred_element_type=_F32)


def _dot_nt(a, b):
    return lax.dot_general(a, b, (((1,), (1,)), ((), ())), preferred_element_type=_F32)


def _dot_tn(a, b):
    return lax.dot_general(a, b, (((0,), (0,)), ((), ())), preferred_element_type=_F32)


def _split2(x):
    hi = x.astype(_BF16)
    lo = (x - hi.astype(_F32)).astype(_BF16)
    return hi, lo


def _split3(x):
    hi = x.astype(_BF16)
    r = x - hi.astype(_F32)
    mid = r.astype(_BF16)
    lo = (r - mid.astype(_F32)).astype(_BF16)
    return hi, mid, lo


def _dot_precise(a, b):
    ah, al = _split2(a)
    bh, bl = _split2(b)
    return _dot(ah, bh) + (_dot(ah, bl) + _dot(al, bh))


def _dot_precise_lhs(a, b_bf16):
    hi, mid, lo = _split3(a)
    return _dot(hi, b_bf16) + (_dot(mid, b_bf16) + _dot(lo, b_bf16))


def _rms(x, g):
    return x * lax.rsqrt(jnp.mean(x * x, axis=-1, keepdims=True) + RMS_EPS) * g


def _ffn_body(x_ref, g_ref, wgu_ref, wd_ref, o_ref, *, d_ff, fc):
    x = x_ref[...]
    xb = _rms(x, g_ref[...]).astype(_BF16)
    acc = None
    for c in range(d_ff // fc):
        g = _dot(xb, wgu_ref[:, c * fc:(c + 1) * fc])
        u = _dot(xb, wgu_ref[:, d_ff + c * fc:d_ff + (c + 1) * fc])
        a = (g * jax.nn.sigmoid(g) * u).astype(_BF16)
        part = _dot(a, wd_ref[c * fc:(c + 1) * fc, :])
        acc = part if acc is None else acc + part
    o_ref[...] = x + 0.5 * acc


def _row_tile(m):
    return min(m, 512)


def _ffn(x, g, wgu, wd, layer):
    m, d = x.shape
    d_ff = wd.shape[1]
    tm = _row_tile(m)
    return pl.pallas_call(
        functools.partial(_ffn_body, d_ff=d_ff, fc=256),
        grid=(m // tm,),
        in_specs=[
            pl.BlockSpec((tm, d), lambda i: (i, 0)),
            _resident((None, 1, d), lambda i: (layer, 0, 0)),
            _resident((None, d, 2 * d_ff), lambda i: (layer, 0, 0)),
            _resident((None, d_ff, d), lambda i: (layer, 0, 0)),
        ],
        out_specs=pl.BlockSpec((tm, d), lambda i: (i, 0)),
        out_shape=jax.ShapeDtypeStruct((m, d), _F32),
        compiler_params=_params("parallel"),
        name="ffn",
    )(x, g, wgu, wd)


def _rotary_apply(x, c, s):
    w = x.shape[1]
    half = DK_RET // 2
    lane = lax.broadcasted_iota(jnp.int32, x.shape, 1)
    up = pltpu.roll(x, w - half, 1)
    dn = pltpu.roll(x, half, 1)
    partner = jnp.where((lane % DK_RET) < half, up, dn)
    return x * c + partner * s


def _inproj_body(h_ref, g_ref, w_ref, wt_ref, cos_ref, sin_ref, *out_refs, token_major_kv):
    ub = _rms(h_ref[...], g_ref[...]).astype(_BF16)

    def proj(n):
        return _dot(ub, w_ref[:, _IN_OFFS[n]:_IN_OFFS[n + 1]])

    mq_ref, rq_ref, rk_ref, rv_ref, rg_ref, sq_ref = out_refs[:6]
    kvt_refs = out_refs[6:10]
    rest = out_refs[10:]
    mq_ref[...] = proj(0)
    c = cos_ref[...]
    s = sin_ref[...]
    rq_ref[...] = _rotary_apply(proj(3), c, s)
    rk_ref[...] = _rotary_apply(proj(4), c, s) * (DK_RET ** -0.5)
    rv_ref[...] = proj(5)
    rg_ref[...] = proj(6)
    sq_ref[...] = proj(7)
    bw = BRANCH_WIDTH
    for n in range(4):
        t = _dot_nt(wt_ref[n * bw:(n + 1) * bw, :], ub)
        groups, _, lw = kvt_refs[n].shape
        for gi in range(groups):
            kvt_refs[n][gi] = t[:, gi * lw:(gi + 1) * lw]
        if token_major_kv:
            rest[n][...] = proj(_KV_COLS[n])
        else:
            for cb in range(rest[n].shape[1]):
                rest[n][0, cb] = t[:, cb * MOBA_BLOCK:(cb + 1) * MOBA_BLOCK].astype(_BF16)


def _inproj(h, g, w_in, w_kvt, cos_t, sin_t, layer, groups, group_len, token_major_kv):
    m, d = h.shape
    tm = _row_tile(m)
    pos_tiles = cos_t.shape[0] // tm
    bw = BRANCH_WIDTH
    rw = H_RET * DK_RET
    row = lambda w: pl.BlockSpec((tm, w), lambda i: (i, 0))
    widths = (bw, rw, rw, bw, bw, bw)
    out_shape = [jax.ShapeDtypeStruct((m, w), _F32) for w in widths]
    out_specs = [row(w) for w in widths]
    if group_len >= tm:
        tiles_per_group = group_len // tm
        kvt_spec = pl.BlockSpec((1, bw, tm), lambda i: (i // tiles_per_group, 0, i % tiles_per_group))
    else:
        assert m == tm, "short groups need all rows in one tile"
        kvt_spec = pl.BlockSpec((groups, bw, group_len), lambda i: (0, 0, 0))
    out_shape += [jax.ShapeDtypeStruct((groups, bw, group_len), _F32)] * 4
    out_specs += [kvt_spec] * 4
    if token_major_kv:
        out_shape += [jax.ShapeDtypeStruct((m, bw), _F32)] * 4
        out_specs += [row(bw)] * 4
    else:
        nb = group_len // MOBA_BLOCK
        tb = tm // MOBA_BLOCK
        out_shape += [jax.ShapeDtypeStruct((groups, nb, bw, MOBA_BLOCK), _BF16)] * 4
        out_specs += [pl.BlockSpec((1, tb, bw, MOBA_BLOCK),
                                   lambda i: (i // tiles_per_group, i % tiles_per_group, 0, 0))] * 4
    return pl.pallas_call(
        functools.partial(_inproj_body, token_major_kv=token_major_kv),
        grid=(m // tm,),
        in_specs=[
            row(d),
            _resident((None, 1, d), lambda i: (layer, 0, 0)),
            _resident((None, d, w_in.shape[2]), lambda i: (layer, 0, 0)),
            _resident((None, 4 * bw, d), lambda i: (layer, 0, 0)),
            pl.BlockSpec((tm, rw), lambda i: (i % pos_tiles, 0)),
            pl.BlockSpec((tm, rw), lambda i: (i % pos_tiles, 0)),
        ],
        out_specs=out_specs,
        out_shape=out_shape,
        compiler_params=_params("parallel"),
        name="inproj",
    )(h, g, w_in, w_kvt, cos_t, sin_t)


def _mixout_body(h_ref, g_ref, ym_ref, yr_ref, ys_ref, wg_ref, bg_ref, wb_ref, wo_ref, o_ref):
    h = h_ref[...]
    d = h.shape[1]
    ub = _rms(h, g_ref[...]).astype(_BF16)
    merged = None
    for n, y_ref in enumerate((ym_ref, yr_ref, ys_ref)):
        gate = jax.nn.sigmoid(_dot(ub, wg_ref[:, n * d:(n + 1) * d]) + bg_ref[:, n * d:(n + 1) * d])
        term = gate * _dot(y_ref[...], wb_ref[n])
        merged = term if merged is None else merged + term
    o_ref[...] = h + _dot(merged.astype(_BF16), wo_ref[...])


def _mixout(h, g, ym, yr, ys, w_gate, b_gate, w_branch, w_out, layer):
    m, d = h.shape
    tm = _row_tile(m)
    bw = BRANCH_WIDTH
    row = lambda w: pl.BlockSpec((tm, w), lambda i: (i, 0))
    return pl.pallas_call(
        _mixout_body,
        grid=(m // tm,),
        in_specs=[
            row(d),
            _resident((None, 1, d), lambda i: (layer, 0, 0)),
            row(bw), row(bw), row(bw),
            _resident((None, d, N_BRANCH * d), lambda i: (layer, 0, 0)),
            _resident((None, 1, N_BRANCH * d), lambda i: (layer, 0, 0)),
            _resident((None, N_BRANCH, bw, d), lambda i: (layer, 0, 0, 0)),
            _resident((None, d, d), lambda i: (layer, 0, 0)),
        ],
        out_specs=row(d),
        out_shape=jax.ShapeDtypeStruct((m, d), _F32),
        compiler_params=_params("parallel"),
        name="mixout",
    )(h, g, ym, yr, ys, w_gate, b_gate, w_branch, w_out)


def _ple_body(h_ref, g_ref, p_ref, wg_ref, wp_ref, fg_ref, o_ref, *, final):
    h = h_ref[...]
    ub = _rms(h, g_ref[...]).astype(_BF16)
    gate = jax.nn.sigmoid(_dot(ub, wg_ref[...]))
    out = h + gate * _dot(p_ref[...].astype(_BF16), wp_ref[...])
    if final:
        out = _rms(out, fg_ref[...])
    o_ref[...] = out


def _ple(h, g, p, w_gate, w_proj, final_g, layer, final):
    m, d = h.shape
    tm = _row_tile(m)
    dp = p.shape[1]
    row = lambda w: pl.BlockSpec((tm, w), lambda i: (i, 0))
    return pl.pallas_call(
        functools.partial(_ple_body, final=final),
        grid=(m // tm,),
        in_specs=[
            row(d),
            _resident((None, 1, d), lambda i: (layer, 0, 0)),
            row(dp),
            _resident((None, d, d), lambda i: (layer, 0, 0)),
            _resident((None, dp, d), lambda i: (layer, 0, 0)),
            _resident((1, d), lambda i: (0, 0)),
        ],
        out_specs=row(d),
        out_shape=jax.ShapeDtypeStruct((m, d), _F32),
        compiler_params=_params("parallel"),
        name="ple",
    )(h, g, p, w_gate, w_proj, final_g)


def _top_blocks(gate, blkf, own):
    valid = blkf < own
    g = jnp.where(valid, gate, jnp.finfo(_F32).min)
    sel = jnp.zeros(gate.shape, _F32)
    for _ in range(MOBA_TOPK):
        m = jnp.max(g, axis=1, keepdims=True)
        first = jnp.min(jnp.where(g == m, blkf, float(LANES)), axis=1, keepdims=True)
        pick = blkf == first
        sel = jnp.where(jnp.logical_and(pick, valid), 1.0, sel)
        g = jnp.where(pick, _NEG_INF, g)
    return sel


def _head_mask(hh):
    lane = lax.broadcasted_iota(jnp.int32, (1, LANES), 1)
    return (lane // HEAD_DIM) == hh


def _upper_tri(n):
    r = lax.broadcasted_iota(jnp.int32, (n, n), 0)
    c = lax.broadcasted_iota(jnp.int32, (n, n), 1)
    return jnp.where(r > c, 1.0, 0.0).astype(_BF16)


def _stick_weights(z, u_tri, carry, strict):
    sp = jnp.log1p(jnp.exp(-jnp.abs(z)))
    log_keep = -(jnp.maximum(z, 0.0) + sp)
    log_beta = jnp.minimum(z, 0.0) - sp
    if strict is not None:
        log_keep = jnp.where(strict, log_keep, 0.0)
    later = _dot_precise_lhs(log_keep, u_tri) + carry
    a = jnp.exp(log_beta + later)
    if strict is not None:
        a = jnp.where(strict, a, 0.0)
    return a, carry + jnp.sum(log_keep, axis=1, keepdims=True)


def _moba_prompt_body(q_ref, k_ref, v_ref, o_ref, kmt):
    i = pl.program_id(2)
    blk = MOBA_BLOCK
    nb = k_ref.shape[1]

    @pl.when(i == 0)
    def _block_means():
        mean_w = jnp.full((blk, LANES), 1.0 / blk, _BF16)
        lane = lax.broadcasted_iota(jnp.int32, (LANES, LANES), 1)
        acc = jnp.zeros((LANES, LANES), _F32)
        for j in range(nb):
            acc = jnp.where(lane == j, _dot(k_ref[0, j], mean_w), acc)
        kmt[...] = acc

    q_raw = q_ref[0]
    row = lax.broadcasted_iota(jnp.int32, (blk, blk), 0)
    col = lax.broadcasted_iota(jnp.int32, (blk, blk), 1)
    causal = col <= row
    blkf = lax.broadcasted_iota(jnp.int32, (blk, LANES), 1).astype(_F32)
    own_f = i.astype(_F32)
    outs = []
    for hh in range(HEADS_PER_LANE_TILE):
        qh = jnp.where(_head_mask(hh), q_raw, 0.0)
        sel = _top_blocks(_dot_precise(qh, kmt[...]), blkf, own_f)
        qb = (qh * _SCALE).astype(_BF16)
        s = jnp.where(causal, _dot(qb, k_ref[0, i]), _NEG_INF)
        m = jnp.max(s, axis=1, keepdims=True)
        p = jnp.exp(s - m)
        l = jnp.sum(p, axis=1, keepdims=True)
        acc = _dot_nt(p.astype(_BF16), v_ref[0, i])

        def body(j, carry):
            m, l, acc = carry
            selj = jnp.max(jnp.where(blkf == j.astype(_F32), sel, 0.0), axis=1, keepdims=True) > 0.0
            s = jnp.where(selj, _dot(qb, k_ref[0, j]), _NEG_INF)
            m_new = jnp.maximum(m, jnp.max(s, axis=1, keepdims=True))
            alpha = jnp.exp(m - m_new)
            p = jnp.exp(s - m_new)
            l = alpha * l + jnp.sum(p, axis=1, keepdims=True)
            acc = alpha * acc + _dot_nt(p.astype(_BF16), v_ref[0, j])
            return m_new, l, acc

        m, l, acc = lax.fori_loop(0, i, body, (m, l, acc))
        outs.append(acc / l)
    o_ref[0] = jnp.where(_head_mask(0), outs[0], outs[1]).astype(o_ref.dtype)


def _kv_block_spec(nb):
    return pl.BlockSpec((1, nb, LANES, MOBA_BLOCK), lambda bi, hp, i: (bi, 0, hp, 0))


def _moba_prompt(q, kt, vt):
    b, t, w = q.shape
    blk = MOBA_BLOCK
    nb = t // blk
    assert nb <= LANES
    return pl.pallas_call(
        _moba_prompt_body,
        grid=(b, w // LANES, nb),
        in_specs=[
            pl.BlockSpec((1, blk, LANES), lambda bi, hp, i: (bi, i, hp)),
            _kv_block_spec(nb),
            _kv_block_spec(nb),
        ],
        out_specs=pl.BlockSpec((1, blk, LANES), lambda bi, hp, i: (bi, i, hp)),
        out_shape=jax.ShapeDtypeStruct((b, t, w), _BF16),
        scratch_shapes=[pltpu.VMEM((LANES, LANES), _F32)],
        compiler_params=_params("parallel", "parallel", "arbitrary"),
        name="moba_prompt",
    )(q, kt, vt)


def _sb_prompt_body(q_ref, k_ref, v_ref, o_ref):
    i = pl.program_id(2)
    tq = MOBA_BLOCK
    q_raw = q_ref[0]
    row = lax.broadcasted_iota(jnp.int32, (tq, tq), 0)
    col = lax.broadcasted_iota(jnp.int32, (tq, tq), 1)
    strict = col < row
    u_tri = _upper_tri(tq)
    outs = []
    for hh in range(HEADS_PER_LANE_TILE):
        qb = (jnp.where(_head_mask(hh), q_raw, 0.0) * _SCALE).astype(_BF16)
        a, carry = _stick_weights(_dot(qb, k_ref[0, i]), u_tri, jnp.zeros((tq, 1), _F32), strict)
        acc = _dot_nt(a.astype(_BF16), v_ref[0, i])

        def body(n, c):
            carry, acc = c
            j = i - 1 - n
            a, carry = _stick_weights(_dot(qb, k_ref[0, j]), u_tri, carry, None)
            return carry, acc + _dot_nt(a.astype(_BF16), v_ref[0, j])

        carry, acc = lax.fori_loop(0, i, body, (carry, acc))
        outs.append(acc)
    o_ref[0] = jnp.where(_head_mask(0), outs[0], outs[1]).astype(o_ref.dtype)


def _sb_prompt(q, kt, vt):
    b, t, w = q.shape
    tq = MOBA_BLOCK
    nb = t // tq
    return pl.pallas_call(
        _sb_prompt_body,
        grid=(b, w // LANES, nb),
        in_specs=[
            pl.BlockSpec((1, tq, LANES), lambda bi, hp, i: (bi, i, hp)),
            _kv_block_spec(nb),
            _kv_block_spec(nb),
        ],
        out_specs=pl.BlockSpec((1, tq, LANES), lambda bi, hp, i: (bi, i, hp)),
        out_shape=jax.ShapeDtypeStruct((b, t, w), _BF16),
        compiler_params=_params("parallel", "parallel", "arbitrary"),
        name="sb_prompt",
    )(q, kt, vt)


def _ret_body(rq_ref, rk_ref, rv_ref, rg_ref, s0_ref, dm_ref, qd_ref, kd_ref, cd_ref, y_ref, sf_ref, state):
    n = pl.program_id(1)

    @pl.when(n == 0)
    def _init():
        for h in range(H_RET):
            off = DK_RET * (h % HEADS_PER_LANE_TILE)
            state[h] = jnp.zeros((LANES, DV_RET), _F32)
            state[h, off:off + DK_RET, :] = s0_ref[0, h]

    for pair in range(H_RET // HEADS_PER_LANE_TILE):
        qp = rq_ref[0, :, pair * LANES:(pair + 1) * LANES]
        kp = rk_ref[0, :, pair * LANES:(pair + 1) * LANES]
        kpb = kp.astype(_BF16)
        for hh in range(HEADS_PER_LANE_TILE):
            h = pair * HEADS_PER_LANE_TILE + hh
            hm = _head_mask(hh)
            qm = jnp.where(hm, qp, 0.0).astype(_BF16)
            inner = _dot_nt(qm, kpb) * dm_ref[h]
            vb = rv_ref[0, :, h * DV_RET:(h + 1) * DV_RET].astype(_BF16)
            s = state[h]
            o = _dot(inner.astype(_BF16), vb) + _dot(qm, s.astype(_BF16)) * qd_ref[h]
            kdec = (jnp.where(hm, kp, 0.0) * kd_ref[h]).astype(_BF16)
            state[h] = s * cd_ref[h] + _dot_tn(kdec, vb)
            on = o * lax.rsqrt(jnp.mean(o * o, axis=-1, keepdims=True) + RMS_EPS)
            g = rg_ref[0, :, h * DV_RET:(h + 1) * DV_RET]
            y_ref[0, :, h * DV_RET:(h + 1) * DV_RET] = (on * (g * jax.nn.sigmoid(g))).astype(y_ref.dtype)

    @pl.when(n == pl.num_programs(1) - 1)
    def _fin():
        for h in range(H_RET):
            off = DK_RET * (h % HEADS_PER_LANE_TILE)
            sf_ref[0, h] = state[h, off:off + DK_RET, :]


def _ret_consts(c):
    log_g = jnp.log(1.0 - 2.0 ** (-5.0 - jnp.arange(H_RET, dtype=_F32)))
    i = jnp.arange(c, dtype=_F32)
    diff = i[:, None] - i[None, :]
    dm = jnp.where(diff >= 0, jnp.exp(log_g[:, None, None] * jnp.maximum(diff, 0.0)), 0.0)
    qd = jnp.broadcast_to(jnp.exp(log_g[:, None] * (i[None, :] + 1.0))[:, :, None], (H_RET, c, DV_RET))
    kd = jnp.broadcast_to(jnp.exp(log_g[:, None] * (c - 1.0 - i[None, :]))[:, :, None], (H_RET, c, LANES))
    cd = jnp.broadcast_to(jnp.exp(log_g * c)[:, None, None], (H_RET, 1, DV_RET))
    return dm, qd, kd, cd


def _retention(rq, rk, rv, rg, s0, c):
    b, t, _ = rq.shape
    n = t // c
    dm, qd, kd, cd = _ret_consts(c)
    rw = H_RET * DK_RET
    bw = BRANCH_WIDTH
    const = lambda a: _resident(a.shape, lambda bi, ni: (0,) * a.ndim)
    return pl.pallas_call(
        _ret_body,
        grid=(b, n),
        in_specs=[
            pl.BlockSpec((1, c, rw), lambda bi, ni: (bi, ni, 0)),
            pl.BlockSpec((1, c, rw), lambda bi, ni: (bi, ni, 0)),
            pl.BlockSpec((1, c, bw), lambda bi, ni: (bi, ni, 0)),
            pl.BlockSpec((1, c, bw), lambda bi, ni: (bi, ni, 0)),
            pl.BlockSpec((1, H_RET, DK_RET, DV_RET), lambda bi, ni: (bi, 0, 0, 0)),
            const(dm), const(qd), const(kd), const(cd),
        ],
        out_specs=[
            pl.BlockSpec((1, c, bw), lambda bi, ni: (bi, ni, 0)),
            pl.BlockSpec((1, H_RET, DK_RET, DV_RET), lambda bi, ni: (bi, 0, 0, 0)),
        ],
        out_shape=[
            jax.ShapeDtypeStruct((b, t, bw), _BF16),
            jax.ShapeDtypeStruct((b, H_RET, DK_RET, DV_RET), _F32),
        ],
        scratch_shapes=[pltpu.VMEM((H_RET, LANES, DV_RET), _F32)],
        compiler_params=_params("parallel", "arbitrary"),
        name="retention",
    )(rq, rk, rv, rg, s0, dm, qd, kd, cd)


NEW_ROWS = 16


def _gather_heads(o, n_tok):
    r, w = o.shape
    hrow = lax.broadcasted_iota(jnp.int32, (r, w), 0) % H_MOBA
    hlane = lax.broadcasted_iota(jnp.int32, (r, w), 1) // HEAD_DIM
    om = jnp.where(hrow == hlane, o, 0.0)
    return jnp.sum(om.reshape(n_tok, H_MOBA, w), axis=1)


def _page_pair_t(p0_ref, p1_ref):
    w = p0_ref.shape[0] * p0_ref.shape[1]
    pg = p0_ref.shape[2]
    return jnp.concatenate([p0_ref[...].reshape(w, pg), p1_ref[...].reshape(w, pg)], axis=1).astype(_BF16)


def _new_rows(ref):
    x = ref[0]
    return jnp.concatenate([x, jnp.zeros((LANES - x.shape[0], x.shape[1]), _F32)], axis=0).astype(_BF16)


def _sample_body(pt_ref, qm_ref, qs_ref, mkn_ref, mvn_ref, skn_ref, svn_ref,
                 mk0_ref, mk1_ref, mv0_ref, mv1_ref, sk0_ref, sk1_ref, sv0_ref, sv1_ref,
                 ym_ref, ys_ref, kmt, m_out, m_max, m_sum, s_carry, s_acc, *, n_blk, n_tok):
    del pt_ref
    j = pl.program_id(1)
    jb = n_blk - 1 - j
    blk = MOBA_BLOCK
    r = qm_ref.shape[1]
    qmb = (qm_ref[0] * _SCALE).astype(_BF16)
    qsb = (qs_ref[0] * _SCALE).astype(_BF16)
    tok = lax.broadcasted_iota(jnp.int32, (r, LANES), 0) // H_MOBA
    colo = lax.broadcasted_iota(jnp.int32, (r, LANES), 1)

    @pl.when(j == 0)
    def _own_sb():
        a, carry = _stick_weights(_dot_nt(qsb, _new_rows(skn_ref)), _upper_tri(LANES),
                                  jnp.zeros((r, 1), _F32), colo < tok)
        s_carry[...] = jnp.broadcast_to(carry, s_carry.shape)
        s_acc[...] = _dot(a.astype(_BF16), _new_rows(svn_ref))
        kmt[...] = jnp.zeros(kmt.shape, _F32)

    a, carry = _stick_weights(_dot(qsb, _page_pair_t(sk0_ref, sk1_ref)), _upper_tri(blk),
                              s_carry[:, 0:1], None)
    s_carry[...] = jnp.broadcast_to(carry, s_carry.shape)
    s_acc[...] += _dot_nt(a.astype(_BF16), _page_pair_t(sv0_ref, sv1_ref))

    kt = _page_pair_t(mk0_ref, mk1_ref)
    s = _dot(qmb, kt)
    m = jnp.max(s, axis=1, keepdims=True)
    p = jnp.exp(s - m)
    m_out[jb] = _dot_nt(p.astype(_BF16), _page_pair_t(mv0_ref, mv1_ref))
    m_max[jb] = jnp.broadcast_to(m, (r, LANES))
    m_sum[jb] = jnp.broadcast_to(jnp.sum(p, axis=1, keepdims=True), (r, LANES))
    lane = lax.broadcasted_iota(jnp.int32, (blk, LANES), 1)
    kmt[...] += _dot(kt, jnp.where(lane == jb, 1.0 / blk, 0.0).astype(_BF16))

    @pl.when(j == n_blk - 1)
    def _finish():
        ys_ref[0] = _gather_heads(s_acc[...], n_tok).astype(ys_ref.dtype)
        blkf = colo.astype(_F32)
        sel = _top_blocks(_dot_precise(qm_ref[0], kmt[...]), blkf, float(n_blk))
        s_own = jnp.where(colo <= tok, _dot_nt(qmb, _new_rows(mkn_ref)), _NEG_INF)
        m_own = jnp.max(s_own, axis=1, keepdims=True)
        picked = [sel[:, b:b + 1] > 0.0 for b in range(n_blk)]
        m_all = m_own
        for b in range(n_blk):
            m_all = jnp.maximum(m_all, jnp.where(picked[b], m_max[b][:, 0:1], _NEG_INF))
        p_own = jnp.exp(s_own - m_all)
        l = jnp.sum(p_own, axis=1, keepdims=True)
        acc = _dot(p_own.astype(_BF16), _new_rows(mvn_ref))
        for b in range(n_blk):
            wgt = jnp.where(picked[b], jnp.exp(m_max[b][:, 0:1] - m_all), 0.0)
            l = l + wgt * m_sum[b][:, 0:1]
            acc = acc + wgt * m_out[b]
        ym_ref[0] = _gather_heads(acc / l, n_tok).astype(ym_ref.dtype)


def _sample_attend(page_table, qm, qs, mkn, mvn, skn, svn, caches, layer):
    b, r, w = qm.shape
    n_tok = r // H_MOBA
    _, _, nh, hd, pg = caches[0].shape
    n_blk = page_table.shape[1] * pg // MOBA_BLOCK
    ppb = MOBA_BLOCK // pg
    assert ppb == 2 and n_blk <= LANES
    page = lambda k: pl.BlockSpec(
        (None, None, nh, hd, pg), lambda bi, j, pt: (layer, pt[bi, ppb * (n_blk - 1 - j) + k], 0, 0, 0))
    seq = lambda rows: pl.BlockSpec((1, rows, w), lambda bi, j, pt: (bi, 0, 0))
    cache_specs = [page(0), page(1)] * 4
    cache_args = [c for c in caches for _ in range(2)]
    out_spec = pl.BlockSpec((1, n_tok, w), lambda bi, j, pt: (bi, 0, 0))
    return pl.pallas_call(
        functools.partial(_sample_body, n_blk=n_blk, n_tok=n_tok),
        grid_spec=pltpu.PrefetchScalarGridSpec(
            num_scalar_prefetch=1,
            grid=(b, n_blk),
            in_specs=[seq(r), seq(r)] + [seq(NEW_ROWS)] * 4 + cache_specs,
            out_specs=[out_spec, out_spec],
            scratch_shapes=[
                pltpu.VMEM((w, LANES), _F32),
                pltpu.VMEM((n_blk, r, w), _F32),
                pltpu.VMEM((n_blk, r, LANES), _F32),
                pltpu.VMEM((n_blk, r, LANES), _F32),
                pltpu.VMEM((r, LANES), _F32),
                pltpu.VMEM((r, w), _F32),
            ],
        ),
        out_shape=[jax.ShapeDtypeStruct((b, n_tok, w), _F32)] * 2,
        compiler_params=_params("parallel", "arbitrary"),
        name="sample_attend",
    )(page_table, qm, qs, mkn, mvn, skn, svn, *cache_args)


def _rotary_tables(pos, reps):
    inv = ROPE_BASE ** (-jnp.arange(0, DK_RET, 2, dtype=_F32) / DK_RET)
    ang = pos.astype(_F32)[:, None] * inv[None, :]
    cos = jnp.cos(ang)
    sin = jnp.sin(ang)
    c = jnp.tile(jnp.concatenate([cos, cos], axis=1), (reps, H_RET))
    s = jnp.tile(jnp.concatenate([-sin, sin], axis=1), (reps, H_RET))
    return c, s


def _block_diag_queries(q):
    b, n_tok, w = q.shape
    head = jnp.arange(w, dtype=jnp.int32) // HEAD_DIM
    keep = head[None, :] == jnp.arange(H_MOBA, dtype=jnp.int32)[:, None]
    return jnp.where(keep[None, None], q[:, :, None, :], 0.0).reshape(b, n_tok * H_MOBA, w)


def kernel(x_prompt, x_sample, cache_moba_k, cache_moba_v, cache_sb_k, cache_sb_v, state_ret, page_table,
           p_prompt, p_sample, ffa_norm, ffa_w_gu, ffa_w_down, mix_norm, w_in, w_gate, b_gate, w_branch,
           w_out, ffb_norm, ffb_w_gu, ffb_w_down, ple_norm, ple_w_gate, ple_w_proj, final_norm):
    bp, tp, d = x_prompt.shape
    bs, ts, _ = x_sample.shape
    depth = w_in.shape[0]
    pg = cache_moba_k.shape[2]
    past_len = page_table.shape[1] * pg
    bw = BRANCH_WIDTH

    bf = lambda a: a.astype(_BF16)
    vec = lambda a: a.reshape(depth, 1, a.shape[-1])
    ffa_gu, ffa_dn, ffb_gu, ffb_dn = bf(ffa_w_gu), bf(ffa_w_down), bf(ffb_w_gu), bf(ffb_w_down)
    w_in_b, w_gate_b, w_branch_b, w_out_b = bf(w_in), bf(w_gate), bf(w_branch), bf(w_out)
    w_kvt = bf(jnp.concatenate([w_in[:, :, _IN_OFFS[n]:_IN_OFFS[n + 1]] for n in _KV_COLS], axis=2)
               .transpose(0, 2, 1))
    ple_g_b, ple_p_b = bf(ple_w_gate), bf(ple_w_proj)
    ffa_n, mix_n, ffb_n, ple_n, b_gate3 = vec(ffa_norm), vec(mix_norm), vec(ffb_norm), vec(ple_norm), vec(b_gate)
    final_g = final_norm.reshape(1, d)

    caches = [c.transpose(0, 1, 3, 4, 2) for c in (cache_moba_k, cache_moba_v, cache_sb_k, cache_sb_v)]

    mp, ms = bp * tp, bs * ts
    cos_p, sin_p = _rotary_tables(jnp.arange(tp, dtype=jnp.int32), 1)
    pos_s = jnp.repeat(past_len + jnp.arange(ts, dtype=jnp.int32), bs)
    cos_s, sin_s = _rotary_tables(pos_s, 1)

    hp = x_prompt.reshape(mp, d)
    hs = x_sample.transpose(1, 0, 2).reshape(ms, d)
    s0_p = jnp.zeros((bp, H_RET, DK_RET, DV_RET), _F32)
    c_p = math.gcd(tp, RET_CHUNK)
    c_s = math.gcd(ts, RET_CHUNK)
    new_p = [[] for _ in range(5)]
    new_s = [[] for _ in range(5)]

    for i in range(depth):
        final = i == depth - 1

        def pre(h, cos_t, sin_t, groups, group_len, token_major_kv):
            h = _ffn(h, ffa_n, ffa_gu, ffa_dn, i)
            return h, _inproj(h, mix_n, w_in_b, w_kvt, cos_t, sin_t, i, groups, group_len, token_major_kv)

        def post(h, ym, yr, ys, p):
            h = _mixout(h, mix_n, ym, yr, ys, w_gate_b, b_gate3, w_branch_b, w_out_b, i)
            h = _ffn(h, ffb_n, ffb_gu, ffb_dn, i)
            return _ple(h, ple_n, p, ple_g_b, ple_p_b, final_g, i, final)

        hp, (mq, rq, rk, rv, rg, sq, mkt, mvt, skt, svt, mkb, mvb, skb, svb) = pre(hp, cos_p, sin_p, bp, tp, False)
        r3 = lambda a: a.reshape(bp, tp, a.shape[-1])
        ym = _moba_prompt(r3(mq), mkb, mvb)
        yr, st = _retention(r3(rq), r3(rk), r3(rv), r3(rg), s0_p, c_p)
        ys = _sb_prompt(r3(sq), skb, svb)
        hp = post(hp, ym.reshape(mp, bw), yr.reshape(mp, bw), ys.reshape(mp, bw), p_prompt[i].reshape(mp, -1))
        for lst, a in zip(new_p, (mkt, mvt, skt, svt, st)):
            lst.append(a)

        hs, (mq, rq, rk, rv, rg, sq, mkt, mvt, skt, svt, mk, mv, sk, sv) = pre(hs, cos_s, sin_s, ts, bs, True)
        seq_major = lambda a: a.reshape(ts, bs, a.shape[-1]).transpose(1, 0, 2)
        tok_major = lambda a: a.transpose(1, 0, 2).reshape(ms, a.shape[-1])
        new_rows = lambda a: jnp.pad(seq_major(a), ((0, 0), (0, NEW_ROWS - ts), (0, 0)))
        ym, ys = _sample_attend(page_table, _block_diag_queries(seq_major(mq)), _block_diag_queries(seq_major(sq)),
                                new_rows(mk), new_rows(mv), new_rows(sk), new_rows(sv), caches, i)
        yr, st = _retention(seq_major(rq), seq_major(rk), seq_major(rv), seq_major(rg), state_ret[i], c_s)
        hs = post(hs, bf(tok_major(ym)), tok_major(yr), bf(tok_major(ys)),
                  p_sample[i].transpose(1, 0, 2).reshape(ms, -1))
        for lst, a in zip(new_s, (mkt, mvt, skt, svt, st.astype(state_ret.dtype))):
            lst.append(a)

    y_prompt = hp.reshape(bp, tp, d)
    y_sample = hs.reshape(ts, bs, d).transpose(1, 0, 2)
    kv_p = [jnp.stack(l).reshape(depth, bp, H_MOBA, HEAD_DIM, tp).transpose(0, 1, 4, 2, 3) for l in new_p[:4]]
    kv_s = [jnp.stack(l).reshape(depth, ts, H_MOBA, HEAD_DIM, bs).transpose(0, 4, 1, 2, 3) for l in new_s[:4]]
    return (y_prompt, y_sample, *kv_p, jnp.stack(new_p[4]), *kv_s, jnp.stack(new_s[4]))
```

```python
import functools
import math

import jax
import jax.numpy as jnp
from jax import lax
from jax.experimental import pallas as pl
from jax.experimental.pallas import tpu as pltpu

D_MODEL = 1024
HEAD_DIM = 64
BRANCH_WIDTH = D_MODEL // 2
N_BRANCH = 3
H_MOBA = BRANCH_WIDTH // HEAD_DIM
MOBA_BLOCK = 256
MOBA_TOPK = 3
H_RET = 4
DV_RET = BRANCH_WIDTH // H_RET
DK_RET = DV_RET // 2
RET_CHUNK = 128
H_SB = BRANCH_WIDTH // HEAD_DIM
ROPE_BASE = 10000.0
RMS_EPS = 1e-6

LANES = 128
HEADS_PER_LANE_TILE = LANES // HEAD_DIM
VMEM_LIMIT_BYTES = 56 * 1024 * 1024

_IN_SIZES = (H_MOBA * HEAD_DIM,) * 3 + (H_RET * DK_RET,) * 2 + (H_RET * DV_RET,) * 2 + (H_SB * HEAD_DIM,) * 3
_IN_OFFS = tuple(int(sum(_IN_SIZES[:i])) for i in range(len(_IN_SIZES) + 1))
_KV_COLS = (1, 2, 8, 9)

_F32 = jnp.float32
_BF16 = jnp.bfloat16
_SCALE = HEAD_DIM ** -0.5
_NEG_INF = float("-inf")
SB_DEAD_CARRY = -110.0


def _params(*sem):
    return pltpu.CompilerParams(dimension_semantics=sem, vmem_limit_bytes=VMEM_LIMIT_BYTES)


def _resident(shape, index_map):
    return pl.BlockSpec(shape, index_map, pipeline_mode=pl.Buffered(1))


def _dot(a, b):
    return jnp.dot(a, b, preferred_element_type=_F32)


def _dot_nt(a, b):
    return lax.dot_general(a, b, (((1,), (1,)), ((), ())), preferred_element_type=_F32)


def _dot_tn(a, b):
    return lax.dot_general(a, b, (((0,), (0,)), ((), ())), preferred_element_type=_F32)


def _split2(x):
    hi = x.astype(_BF16)
    lo = (x - hi.astype(_F32)).astype(_BF16)
    return hi, lo


def _dot_precise(a, b):
    ah, al = _split2(a)
    bh, bl = _split2(b)
    return _dot(ah, bh) + (_dot(ah, bl) + _dot(al, bh))


def _dot_split_lhs(a, b_bf16):
    hi, lo = _split2(a)
    return _dot(hi, b_bf16) + _dot(lo, b_bf16)


def _rms(x, g):
    return x * lax.rsqrt(jnp.mean(x * x, axis=-1, keepdims=True) + RMS_EPS) * g


def _ffn_body(x_ref, g_ref, wgu_ref, wd_ref, o_ref, *, d_ff, fc):
    x = x_ref[...]
    xb = _rms(x, g_ref[...]).astype(_BF16)
    acc = None
    for c in range(d_ff // fc):
        g = _dot(xb, wgu_ref[:, c * fc:(c + 1) * fc])
        u = _dot(xb, wgu_ref[:, d_ff + c * fc:d_ff + (c + 1) * fc])
        a = (g * jax.nn.sigmoid(g) * u).astype(_BF16)
        part = _dot(a, wd_ref[c * fc:(c + 1) * fc, :])
        acc = part if acc is None else acc + part
    o_ref[...] = x + 0.5 * acc


def _row_tile(m):
    return min(m, 512)


def _ffn(x, g, wgu, wd, layer):
    m, d = x.shape
    d_ff = wd.shape[1]
    tm = _row_tile(m)
    return pl.pallas_call(
        functools.partial(_ffn_body, d_ff=d_ff, fc=256),
        grid=(m // tm,),
        in_specs=[
            pl.BlockSpec((tm, d), lambda i: (i, 0)),
            _resident((None, 1, d), lambda i: (layer, 0, 0)),
            _resident((None, d, 2 * d_ff), lambda i: (layer, 0, 0)),
            _resident((None, d_ff, d), lambda i: (layer, 0, 0)),
        ],
        out_specs=pl.BlockSpec((tm, d), lambda i: (i, 0)),
        out_shape=jax.ShapeDtypeStruct((m, d), _F32),
        compiler_params=_params("parallel"),
        name="ffn",
    )(x, g, wgu, wd)


def _rotary_apply(x, c, s):
    w = x.shape[1]
    half = DK_RET // 2
    lane = lax.broadcasted_iota(jnp.int32, x.shape, 1)
    up = pltpu.roll(x, w - half, 1)
    dn = pltpu.roll(x, half, 1)
    partner = jnp.where((lane % DK_RET) < half, up, dn)
    return x * c + partner * s


def _inproj_body(h_ref, g_ref, w_ref, wt_ref, cos_ref, sin_ref, *out_refs, token_major_kv):
    ub = _rms(h_ref[...], g_ref[...]).astype(_BF16)

    def proj(n):
        return _dot(ub, w_ref[:, _IN_OFFS[n]:_IN_OFFS[n + 1]])

    mq_ref, rq_ref, rk_ref, rv_ref, rg_ref, sq_ref = out_refs[:6]
    kvt_refs = out_refs[6:10]
    rest = out_refs[10:]
    mq_ref[...] = proj(0)
    c = cos_ref[...]
    s = sin_ref[...]
    rq_ref[...] = _rotary_apply(proj(3), c, s)
    rk_ref[...] = _rotary_apply(proj(4), c, s) * (DK_RET ** -0.5)
    rv_ref[...] = proj(5)
    rg_ref[...] = proj(6)
    sq_ref[...] = proj(7)
    bw = BRANCH_WIDTH
    for n in range(4):
        t = _dot_nt(wt_ref[n * bw:(n + 1) * bw, :], ub)
        groups, _, lw = kvt_refs[n].shape
        for gi in range(groups):
            kvt_refs[n][gi] = t[:, gi * lw:(gi + 1) * lw]
        if token_major_kv:
            rest[n][...] = proj(_KV_COLS[n])
        else:
            for cb in range(rest[n].shape[1]):
                rest[n][0, cb] = t[:, cb * MOBA_BLOCK:(cb + 1) * MOBA_BLOCK].astype(_BF16)


def _inproj(h, g, w_in, w_kvt, cos_t, sin_t, layer, groups, group_len, token_major_kv):
    m, d = h.shape
    tm = _row_tile(m)
    pos_tiles = cos_t.shape[0] // tm
    bw = BRANCH_WIDTH
    rw = H_RET * DK_RET
    row = lambda w: pl.BlockSpec((tm, w), lambda i: (i, 0))
    widths = (bw, rw, rw, bw, bw, bw)
    out_shape = [jax.ShapeDtypeStruct((m, w), _F32) for w in widths]
    out_specs = [row(w) for w in widths]
    if group_len >= tm:
        tiles_per_group = group_len // tm
        kvt_spec = pl.BlockSpec((1, bw, tm), lambda i: (i // tiles_per_group, 0, i % tiles_per_group))
    else:
        assert m == tm, "short groups need all rows in one tile"
        kvt_spec = pl.BlockSpec((groups, bw, group_len), lambda i: (0, 0, 0))
    out_shape += [jax.ShapeDtypeStruct((groups, bw, group_len), _F32)] * 4
    out_specs += [kvt_spec] * 4
    if token_major_kv:
        out_shape += [jax.ShapeDtypeStruct((m, bw), _F32)] * 4
        out_specs += [row(bw)] * 4
    else:
        nb = group_len // MOBA_BLOCK
        tb = tm // MOBA_BLOCK
        out_shape += [jax.ShapeDtypeStruct((groups, nb, bw, MOBA_BLOCK), _BF16)] * 4
        out_specs += [pl.BlockSpec((1, tb, bw, MOBA_BLOCK),
                                   lambda i: (i // tiles_per_group, i % tiles_per_group, 0, 0))] * 4
    return pl.pallas_call(
        functools.partial(_inproj_body, token_major_kv=token_major_kv),
        grid=(m // tm,),
        in_specs=[
            row(d),
            _resident((None, 1, d), lambda i: (layer, 0, 0)),
            _resident((None, d, w_in.shape[2]), lambda i: (layer, 0, 0)),
            _resident((None, 4 * bw, d), lambda i: (layer, 0, 0)),
            pl.BlockSpec((tm, rw), lambda i: (i % pos_tiles, 0)),
            pl.BlockSpec((tm, rw), lambda i: (i % pos_tiles, 0)),
        ],
        out_specs=out_specs,
        out_shape=out_shape,
        compiler_params=_params("parallel"),
        name="inproj",
    )(h, g, w_in, w_kvt, cos_t, sin_t)


def _mixout_body(h_ref, g_ref, ym_ref, yr_ref, ys_ref, wg_ref, bg_ref, wb_ref, wo_ref, o_ref):
    h = h_ref[...]
    d = h.shape[1]
    ub = _rms(h, g_ref[...]).astype(_BF16)
    merged = None
    for n, y_ref in enumerate((ym_ref, yr_ref, ys_ref)):
        gate = jax.nn.sigmoid(_dot(ub, wg_ref[:, n * d:(n + 1) * d]) + bg_ref[:, n * d:(n + 1) * d])
        term = gate * _dot(y_ref[...], wb_ref[n])
        merged = term if merged is None else merged + term
    o_ref[...] = h + _dot(merged.astype(_BF16), wo_ref[...])


def _mixout(h, g, ym, yr, ys, w_gate, b_gate, w_branch, w_out, layer):
    m, d = h.shape
    tm = _row_tile(m)
    bw = BRANCH_WIDTH
    row = lambda w: pl.BlockSpec((tm, w), lambda i: (i, 0))
    return pl.pallas_call(
        _mixout_body,
        grid=(m // tm,),
        in_specs=[
            row(d),
            _resident((None, 1, d), lambda i: (layer, 0, 0)),
            row(bw), row(bw), row(bw),
            _resident((None, d, N_BRANCH * d), lambda i: (layer, 0, 0)),
            _resident((None, 1, N_BRANCH * d), lambda i: (layer, 0, 0)),
            _resident((None, N_BRANCH, bw, d), lambda i: (layer, 0, 0, 0)),
            _resident((None, d, d), lambda i: (layer, 0, 0)),
        ],
        out_specs=row(d),
        out_shape=jax.ShapeDtypeStruct((m, d), _F32),
        compiler_params=_params("parallel"),
        name="mixout",
    )(h, g, ym, yr, ys, w_gate, b_gate, w_branch, w_out)


def _ple_body(h_ref, g_ref, p_ref, wg_ref, wp_ref, fg_ref, o_ref, *, final):
    h = h_ref[...]
    ub = _rms(h, g_ref[...]).astype(_BF16)
    gate = jax.nn.sigmoid(_dot(ub, wg_ref[...]))
    out = h + gate * _dot(p_ref[...].astype(_BF16), wp_ref[...])
    if final:
        out = _rms(out, fg_ref[...])
    o_ref[...] = out


def _ple(h, g, p, w_gate, w_proj, final_g, layer, final):
    m, d = h.shape
    tm = _row_tile(m)
    dp = p.shape[1]
    row = lambda w: pl.BlockSpec((tm, w), lambda i: (i, 0))
    return pl.pallas_call(
        functools.partial(_ple_body, final=final),
        grid=(m // tm,),
        in_specs=[
            row(d),
            _resident((None, 1, d), lambda i: (layer, 0, 0)),
            row(dp),
            _resident((None, d, d), lambda i: (layer, 0, 0)),
            _resident((None, dp, d), lambda i: (layer, 0, 0)),
            _resident((1, d), lambda i: (0, 0)),
        ],
        out_specs=row(d),
        out_shape=jax.ShapeDtypeStruct((m, d), _F32),
        compiler_params=_params("parallel"),
        name="ple",
    )(h, g, p, w_gate, w_proj, final_g)


def _top_blocks(gate, blkf, own):
    valid = blkf < own
    g = jnp.where(valid, gate, jnp.finfo(_F32).min)
    sel = jnp.zeros(gate.shape, _F32)
    for _ in range(MOBA_TOPK):
        m = jnp.max(g, axis=1, keepdims=True)
        first = jnp.min(jnp.where(g == m, blkf, float(LANES)), axis=1, keepdims=True)
        pick = blkf == first
        sel = jnp.where(jnp.logical_and(pick, valid), 1.0, sel)
        g = jnp.where(pick, _NEG_INF, g)
    return sel


def _head_mask(hh):
    lane = lax.broadcasted_iota(jnp.int32, (1, LANES), 1)
    return (lane // HEAD_DIM) == hh


def _upper_tri(n):
    r = lax.broadcasted_iota(jnp.int32, (n, n), 0)
    c = lax.broadcasted_iota(jnp.int32, (n, n), 1)
    return jnp.where(r > c, 1.0, 0.0).astype(_BF16)


def _stick_weights(z, u_tri, carry, strict):
    sp = jnp.log(1.0 + jnp.exp(-jnp.abs(z)))
    log_keep = -(jnp.maximum(z, 0.0) + sp)
    log_beta = jnp.minimum(z, 0.0) - sp
    if strict is not None:
        log_keep = jnp.where(strict, log_keep, 0.0)
    later = _dot_split_lhs(log_keep, u_tri) + carry
    a = jnp.exp(log_beta + later)
    if strict is not None:
        a = jnp.where(strict, a, 0.0)
    return a, carry + jnp.sum(log_keep, axis=1, keepdims=True)


def _moba_prompt_body(q_ref, k_ref, v_ref, o_ref, kmt):
    i = pl.program_id(2)
    blk = MOBA_BLOCK
    nb = k_ref.shape[1]

    @pl.when(i == 0)
    def _block_means():
        mean_w = jnp.full((blk, LANES), 1.0 / blk, _BF16)
        lane = lax.broadcasted_iota(jnp.int32, (LANES, LANES), 1)
        acc = jnp.zeros((LANES, LANES), _F32)
        for j in range(nb):
            acc = jnp.where(lane == j, _dot(k_ref[0, j], mean_w), acc)
        kmt[...] = acc

    q_raw = q_ref[0]
    row = lax.broadcasted_iota(jnp.int32, (blk, blk), 0)
    col = lax.broadcasted_iota(jnp.int32, (blk, blk), 1)
    causal = col <= row
    blkf = lax.broadcasted_iota(jnp.int32, (blk, LANES), 1).astype(_F32)
    own_f = i.astype(_F32)
    heads = range(HEADS_PER_LANE_TILE)
    qbs, sels, state = [], [], []
    kt = k_ref[0, i]
    vt = v_ref[0, i]
    for hh in heads:
        qh = jnp.where(_head_mask(hh), q_raw, 0.0)
        sels.append(_top_blocks(_dot_precise(qh, kmt[...]), blkf, own_f).astype(_BF16))
        qb = (qh * _SCALE).astype(_BF16)
        qbs.append(qb)
        s = jnp.where(causal, _dot(qb, kt), _NEG_INF)
        m = jnp.max(s, axis=1, keepdims=True)
        p = jnp.exp(s - m)
        state += [m, jnp.sum(p, axis=1, keepdims=True), _dot_nt(p.astype(_BF16), vt)]
    brow = lax.broadcasted_iota(jnp.int32, (LANES, blk), 0)

    def body(j, st):
        kt = k_ref[0, j]
        vt = v_ref[0, j]
        pick_row = jnp.where(brow == j, 1.0, 0.0).astype(_BF16)
        out = []
        for hh in heads:
            m, l, acc = st[3 * hh:3 * hh + 3]
            s = jnp.where(_dot(sels[hh], pick_row) > 0.5, _dot(qbs[hh], kt), _NEG_INF)
            m_new = jnp.maximum(m, jnp.max(s, axis=1, keepdims=True))
            alpha = jnp.exp(m - m_new)
            p = jnp.exp(s - m_new)
            l = alpha * l + jnp.sum(p, axis=1, keepdims=True)
            acc = alpha * acc + _dot_nt(p.astype(_BF16), vt)
            out += [m_new, l, acc]
        return tuple(out)

    st = lax.fori_loop(0, i, body, tuple(state))
    o_ref[0] = jnp.where(_head_mask(0), st[2] / st[1], st[5] / st[4]).astype(o_ref.dtype)


def _kv_block_spec(nb):
    return pl.BlockSpec((1, nb, LANES, MOBA_BLOCK), lambda bi, hp, i: (bi, 0, hp, 0))


def _moba_prompt(q, kt, vt):
    b, t, w = q.shape
    blk = MOBA_BLOCK
    nb = t // blk
    assert nb <= LANES
    return pl.pallas_call(
        _moba_prompt_body,
        grid=(b, w // LANES, nb),
        in_specs=[
            pl.BlockSpec((1, blk, LANES), lambda bi, hp, i: (bi, i, hp)),
            _kv_block_spec(nb),
            _kv_block_spec(nb),
        ],
        out_specs=pl.BlockSpec((1, blk, LANES), lambda bi, hp, i: (bi, i, hp)),
        out_shape=jax.ShapeDtypeStruct((b, t, w), _BF16),
        scratch_shapes=[pltpu.VMEM((LANES, LANES), _F32)],
        compiler_params=_params("parallel", "parallel", "arbitrary"),
        name="moba_prompt",
    )(q, kt, vt)


def _sb_prompt_body(q_ref, k_ref, v_ref, o_ref):
    i = pl.program_id(2)
    tq = MOBA_BLOCK
    q_raw = q_ref[0]
    row = lax.broadcasted_iota(jnp.int32, (tq, tq), 0)
    col = lax.broadcasted_iota(jnp.int32, (tq, tq), 1)
    strict = col < row
    u_tri = _upper_tri(tq)
    heads = range(HEADS_PER_LANE_TILE)
    qbs = [(jnp.where(_head_mask(hh), q_raw, 0.0) * _SCALE).astype(_BF16) for hh in heads]

    def visit(j, carries, accs, mask):
        kt = k_ref[0, j]
        vt = v_ref[0, j]
        new_c, new_a = [], []
        for hh in heads:
            a, c = _stick_weights(_dot(qbs[hh], kt), u_tri, carries[hh], mask)
            new_c.append(c)
            new_a.append(accs[hh] + _dot_nt(a.astype(_BF16), vt))
        return new_c, new_a

    def live_of(carries):
        return jnp.maximum(jnp.max(carries[0]), jnp.max(carries[1]))

    zero_c = jnp.zeros((tq, 1), _F32)
    zero_a = jnp.zeros((tq, LANES), _F32)
    carries, accs = visit(i, [zero_c, zero_c], [zero_a, zero_a], strict)

    def cond(st):
        return jnp.logical_and(st[0] < i, st[5] > SB_DEAD_CARRY)

    def body(st):
        n, c0, c1, a0, a1, _ = st
        (c0, c1), (a0, a1) = visit(i - 1 - n, [c0, c1], [a0, a1], None)
        return n + 1, c0, c1, a0, a1, live_of([c0, c1])

    st = lax.while_loop(cond, body, (jnp.int32(0), carries[0], carries[1], accs[0], accs[1], live_of(carries)))
    o_ref[0] = jnp.where(_head_mask(0), st[3], st[4]).astype(o_ref.dtype)


def _sb_prompt(q, kt, vt):
    b, t, w = q.shape
    tq = MOBA_BLOCK
    nb = t // tq
    return pl.pallas_call(
        _sb_prompt_body,
        grid=(b, w // LANES, nb),
        in_specs=[
            pl.BlockSpec((1, tq, LANES), lambda bi, hp, i: (bi, i, hp)),
            _kv_block_spec(nb),
            _kv_block_spec(nb),
        ],
        out_specs=pl.BlockSpec((1, tq, LANES), lambda bi, hp, i: (bi, i, hp)),
        out_shape=jax.ShapeDtypeStruct((b, t, w), _BF16),
        compiler_params=_params("parallel", "parallel", "arbitrary"),
        name="sb_prompt",
    )(q, kt, vt)


def _ret_body(rq_ref, rk_ref, rv_ref, rg_ref, s0_ref, dm_ref, qd_ref, kd_ref, cd_ref, y_ref, sf_ref, state):
    n = pl.program_id(1)

    @pl.when(n == 0)
    def _init():
        for h in range(H_RET):
            off = DK_RET * (h % HEADS_PER_LANE_TILE)
            state[h] = jnp.zeros((LANES, DV_RET), _F32)
            state[h, off:off + DK_RET, :] = s0_ref[0, h]

    for pair in range(H_RET // HEADS_PER_LANE_TILE):
        qp = rq_ref[0, :, pair * LANES:(pair + 1) * LANES]
        kp = rk_ref[0, :, pair * LANES:(pair + 1) * LANES]
        kpb = kp.astype(_BF16)
        for hh in range(HEADS_PER_LANE_TILE):
            h = pair * HEADS_PER_LANE_TILE + hh
            hm = _head_mask(hh)
            qm = jnp.where(hm, qp, 0.0).astype(_BF16)
            inner = _dot_nt(qm, kpb) * dm_ref[h]
            vb = rv_ref[0, :, h * DV_RET:(h + 1) * DV_RET].astype(_BF16)
            s = state[h]
            o = _dot(inner.astype(_BF16), vb) + _dot(qm, s.astype(_BF16)) * qd_ref[h]
            kdec = (jnp.where(hm, kp, 0.0) * kd_ref[h]).astype(_BF16)
            state[h] = s * cd_ref[h] + _dot_tn(kdec, vb)
            on = o * lax.rsqrt(jnp.mean(o * o, axis=-1, keepdims=True) + RMS_EPS)
            g = rg_ref[0, :, h * DV_RET:(h + 1) * DV_RET]
            y_ref[0, :, h * DV_RET:(h + 1) * DV_RET] = (on * (g * jax.nn.sigmoid(g))).astype(y_ref.dtype)

    @pl.when(n == pl.num_programs(1) - 1)
    def _fin():
        for h in range(H_RET):
            off = DK_RET * (h % HEADS_PER_LANE_TILE)
            sf_ref[0, h] = state[h, off:off + DK_RET, :]


def _ret_consts(c):
    log_g = jnp.log(1.0 - 2.0 ** (-5.0 - jnp.arange(H_RET, dtype=_F32)))
    i = jnp.arange(c, dtype=_F32)
    diff = i[:, None] - i[None, :]
    dm = jnp.where(diff >= 0, jnp.exp(log_g[:, None, None] * jnp.maximum(diff, 0.0)), 0.0)
    qd = jnp.broadcast_to(jnp.exp(log_g[:, None] * (i[None, :] + 1.0))[:, :, None], (H_RET, c, DV_RET))
    kd = jnp.broadcast_to(jnp.exp(log_g[:, None] * (c - 1.0 - i[None, :]))[:, :, None], (H_RET, c, LANES))
    cd = jnp.broadcast_to(jnp.exp(log_g * c)[:, None, None], (H_RET, 1, DV_RET))
    return dm, qd, kd, cd


def _retention(rq, rk, rv, rg, s0, c):
    b, t, _ = rq.shape
    n = t // c
    dm, qd, kd, cd = _ret_consts(c)
    rw = H_RET * DK_RET
    bw = BRANCH_WIDTH
    const = lambda a: _resident(a.shape, lambda bi, ni: (0,) * a.ndim)
    return pl.pallas_call(
        _ret_body,
        grid=(b, n),
        in_specs=[
            pl.BlockSpec((1, c, rw), lambda bi, ni: (bi, ni, 0)),
            pl.BlockSpec((1, c, rw), lambda bi, ni: (bi, ni, 0)),
            pl.BlockSpec((1, c, bw), lambda bi, ni: (bi, ni, 0)),
            pl.BlockSpec((1, c, bw), lambda bi, ni: (bi, ni, 0)),
            pl.BlockSpec((1, H_RET, DK_RET, DV_RET), lambda bi, ni: (bi, 0, 0, 0)),
            const(dm), const(qd), const(kd), const(cd),
        ],
        out_specs=[
            pl.BlockSpec((1, c, bw), lambda bi, ni: (bi, ni, 0)),
            pl.BlockSpec((1, H_RET, DK_RET, DV_RET), lambda bi, ni: (bi, 0, 0, 0)),
        ],
        out_shape=[
            jax.ShapeDtypeStruct((b, t, bw), _BF16),
            jax.ShapeDtypeStruct((b, H_RET, DK_RET, DV_RET), _F32),
        ],
        scratch_shapes=[pltpu.VMEM((H_RET, LANES, DV_RET), _F32)],
        compiler_params=_params("parallel", "arbitrary"),
        name="retention",
    )(rq, rk, rv, rg, s0, dm, qd, kd, cd)


NEW_ROWS = 16
SAMPLE_BLOCKS_PER_STEP = 4


def _gather_heads(o, n_tok):
    r, w = o.shape
    hrow = lax.broadcasted_iota(jnp.int32, (r, w), 0) % H_MOBA
    hlane = lax.broadcasted_iota(jnp.int32, (r, w), 1) // HEAD_DIM
    om = jnp.where(hrow == hlane, o, 0.0)
    return jnp.sum(om.reshape(n_tok, H_MOBA, w), axis=1)


def _page_pair_t(p0_ref, p1_ref):
    w = p0_ref.shape[0] * p0_ref.shape[1]
    pg = p0_ref.shape[2]
    return jnp.concatenate([p0_ref[...].reshape(w, pg), p1_ref[...].reshape(w, pg)], axis=1).astype(_BF16)


def _new_rows(ref):
    x = ref[0]
    return jnp.concatenate([x, jnp.zeros((LANES - x.shape[0], x.shape[1]), _F32)], axis=0).astype(_BF16)


def _sample_body(pt_ref, qm_ref, qs_ref, mkn_ref, mvn_ref, skn_ref, svn_ref, *refs, n_blk, n_tok, bps):
    del pt_ref
    npg = 2 * bps
    mk_p, mv_p, sk_p, sv_p = (refs[c * npg:(c + 1) * npg] for c in range(4))
    ym_ref, ys_ref, kmt, m_out, m_max, m_sum, s_carry, s_acc = refs[4 * npg:]
    j = pl.program_id(1)
    n_steps = n_blk // bps
    base = (n_steps - 1 - j) * bps
    blk = MOBA_BLOCK
    r = qm_ref.shape[1]
    qmb = (qm_ref[0] * _SCALE).astype(_BF16)
    qsb = (qs_ref[0] * _SCALE).astype(_BF16)
    tok = lax.broadcasted_iota(jnp.int32, (r, LANES), 0) // H_MOBA
    colo = lax.broadcasted_iota(jnp.int32, (r, LANES), 1)

    @pl.when(j == 0)
    def _own_sb():
        a, carry = _stick_weights(_dot_nt(qsb, _new_rows(skn_ref)), _upper_tri(LANES),
                                  jnp.zeros((r, 1), _F32), colo < tok)
        s_carry[...] = jnp.broadcast_to(carry, s_carry.shape)
        s_acc[...] = _dot(a.astype(_BF16), _new_rows(svn_ref))
        kmt[...] = jnp.zeros(kmt.shape, _F32)

    u_tri = _upper_tri(blk)
    lane = lax.broadcasted_iota(jnp.int32, (blk, LANES), 1)
    carry = s_carry[:, 0:1]
    acc_s = s_acc[...]
    kmt_add = None
    for b in range(bps - 1, -1, -1):
        jb = base + b
        a, carry = _stick_weights(_dot(qsb, _page_pair_t(sk_p[2 * b], sk_p[2 * b + 1])), u_tri, carry, None)
        acc_s = acc_s + _dot_nt(a.astype(_BF16), _page_pair_t(sv_p[2 * b], sv_p[2 * b + 1]))
        kt = _page_pair_t(mk_p[2 * b], mk_p[2 * b + 1])
        s = _dot(qmb, kt)
        m = jnp.max(s, axis=1, keepdims=True)
        p = jnp.exp(s - m)
        m_out[jb] = _dot_nt(p.astype(_BF16), _page_pair_t(mv_p[2 * b], mv_p[2 * b + 1]))
        m_max[jb] = jnp.broadcast_to(m, (r, LANES))
        m_sum[jb] = jnp.broadcast_to(jnp.sum(p, axis=1, keepdims=True), (r, LANES))
        part = _dot(kt, jnp.where(lane == jb, 1.0 / blk, 0.0).astype(_BF16))
        kmt_add = part if kmt_add is None else kmt_add + part
    s_carry[...] = jnp.broadcast_to(carry, s_carry.shape)
    s_acc[...] = acc_s
    kmt[...] += kmt_add

    @pl.when(j == n_steps - 1)
    def _finish():
        ys_ref[0] = _gather_heads(s_acc[...], n_tok).astype(ys_ref.dtype)
        blkf = colo.astype(_F32)
        sel = _top_blocks(_dot_precise(qm_ref[0], kmt[...]), blkf, float(n_blk))
        s_own = jnp.where(colo <= tok, _dot_nt(qmb, _new_rows(mkn_ref)), _NEG_INF)
        m_own = jnp.max(s_own, axis=1, keepdims=True)
        picked = [sel[:, b:b + 1] > 0.0 for b in range(n_blk)]
        m_all = m_own
        for b in range(n_blk):
            m_all = jnp.maximum(m_all, jnp.where(picked[b], m_max[b][:, 0:1], _NEG_INF))
        p_own = jnp.exp(s_own - m_all)
        l = jnp.sum(p_own, axis=1, keepdims=True)
        acc = _dot(p_own.astype(_BF16), _new_rows(mvn_ref))
        for b in range(n_blk):
            wgt = jnp.where(picked[b], jnp.exp(m_max[b][:, 0:1] - m_all), 0.0)
            l = l + wgt * m_sum[b][:, 0:1]
            acc = acc + wgt * m_out[b]
        ym_ref[0] = _gather_heads(acc / l, n_tok).astype(ym_ref.dtype)


def _sample_attend(page_table, qm, qs, mkn, mvn, skn, svn, caches, layer):
    b, r, w = qm.shape
    n_tok = r // H_MOBA
    _, _, nh, hd, pg = caches[0].shape
    n_blk = page_table.shape[1] * pg // MOBA_BLOCK
    bps = math.gcd(n_blk, SAMPLE_BLOCKS_PER_STEP)
    n_steps = n_blk // bps
    npg = bps * MOBA_BLOCK // pg
    assert npg == 2 * bps and n_blk <= LANES
    page = lambda k: pl.BlockSpec(
        (None, None, nh, hd, pg), lambda bi, j, pt: (layer, pt[bi, npg * (n_steps - 1 - j) + k], 0, 0, 0))
    seq = lambda rows: pl.BlockSpec((1, rows, w), lambda bi, j, pt: (bi, 0, 0))
    cache_specs = [page(k) for _ in range(4) for k in range(npg)]
    cache_args = [c for c in caches for _ in range(npg)]
    out_spec = pl.BlockSpec((1, n_tok, w), lambda bi, j, pt: (bi, 0, 0))
    return pl.pallas_call(
        functools.partial(_sample_body, n_blk=n_blk, n_tok=n_tok, bps=bps),
        grid_spec=pltpu.PrefetchScalarGridSpec(
            num_scalar_prefetch=1,
            grid=(b, n_steps),
            in_specs=[seq(r), seq(r)] + [seq(NEW_ROWS)] * 4 + cache_specs,
            out_specs=[out_spec, out_spec],
            scratch_shapes=[
                pltpu.VMEM((w, LANES), _F32),
                pltpu.VMEM((n_blk, r, w), _F32),
                pltpu.VMEM((n_blk, r, LANES), _F32),
                pltpu.VMEM((n_blk, r, LANES), _F32),
                pltpu.VMEM((r, LANES), _F32),
                pltpu.VMEM((r, w), _F32),
            ],
        ),
        out_shape=[jax.ShapeDtypeStruct((b, n_tok, w), _F32)] * 2,
        compiler_params=_params("parallel", "arbitrary"),
        name="sample_attend",
    )(page_table, qm, qs, mkn, mvn, skn, svn, *cache_args)


def _rotary_tables(pos, reps):
    inv = ROPE_BASE ** (-jnp.arange(0, DK_RET, 2, dtype=_F32) / DK_RET)
    ang = pos.astype(_F32)[:, None] * inv[None, :]
    cos = jnp.cos(ang)
    sin = jnp.sin(ang)
    c = jnp.tile(jnp.concatenate([cos, cos], axis=1), (reps, H_RET))
    s = jnp.tile(jnp.concatenate([-sin, sin], axis=1), (reps, H_RET))
    return c, s


def _block_diag_queries(q):
    b, n_tok, w = q.shape
    head = jnp.arange(w, dtype=jnp.int32) // HEAD_DIM
    keep = head[None, :] == jnp.arange(H_MOBA, dtype=jnp.int32)[:, None]
    return jnp.where(keep[None, None], q[:, :, None, :], 0.0).reshape(b, n_tok * H_MOBA, w)


def kernel(x_prompt, x_sample, cache_moba_k, cache_moba_v, cache_sb_k, cache_sb_v, state_ret, page_table,
           p_prompt, p_sample, ffa_norm, ffa_w_gu, ffa_w_down, mix_norm, w_in, w_gate, b_gate, w_branch,
           w_out, ffb_norm, ffb_w_gu, ffb_w_down, ple_norm, ple_w_gate, ple_w_proj, final_norm):
    bp, tp, d = x_prompt.shape
    bs, ts, _ = x_sample.shape
    depth = w_in.shape[0]
    pg = cache_moba_k.shape[2]
    past_len = page_table.shape[1] * pg
    bw = BRANCH_WIDTH

    bf = lambda a: a.astype(_BF16)
    vec = lambda a: a.reshape(depth, 1, a.shape[-1])
    ffa_gu, ffa_dn, ffb_gu, ffb_dn = bf(ffa_w_gu), bf(ffa_w_down), bf(ffb_w_gu), bf(ffb_w_down)
    w_in_b, w_gate_b, w_branch_b, w_out_b = bf(w_in), bf(w_gate), bf(w_branch), bf(w_out)
    w_kvt = bf(jnp.concatenate([w_in[:, :, _IN_OFFS[n]:_IN_OFFS[n + 1]] for n in _KV_COLS], axis=2)
               .transpose(0, 2, 1))
    ple_g_b, ple_p_b = bf(ple_w_gate), bf(ple_w_proj)
    ffa_n, mix_n, ffb_n, ple_n, b_gate3 = vec(ffa_norm), vec(mix_norm), vec(ffb_norm), vec(ple_norm), vec(b_gate)
    final_g = final_norm.reshape(1, d)

    caches = [c.transpose(0, 1, 3, 4, 2) for c in (cache_moba_k, cache_moba_v, cache_sb_k, cache_sb_v)]

    mp, ms = bp * tp, bs * ts
    cos_p, sin_p = _rotary_tables(jnp.arange(tp, dtype=jnp.int32), 1)
    pos_s = jnp.repeat(past_len + jnp.arange(ts, dtype=jnp.int32), bs)
    cos_s, sin_s = _rotary_tables(pos_s, 1)

    hp = x_prompt.reshape(mp, d)
    hs = x_sample.transpose(1, 0, 2).reshape(ms, d)
    s0_p = jnp.zeros((bp, H_RET, DK_RET, DV_RET), _F32)
    c_p = math.gcd(tp, RET_CHUNK)
    c_s = math.gcd(ts, RET_CHUNK)
    new_p = [[] for _ in range(5)]
    new_s = [[] for _ in range(5)]

    for i in range(depth):
        final = i == depth - 1

        def pre(h, cos_t, sin_t, groups, group_len, token_major_kv):
            h = _ffn(h, ffa_n, ffa_gu, ffa_dn, i)
            return h, _inproj(h, mix_n, w_in_b, w_kvt, cos_t, sin_t, i, groups, group_len, token_major_kv)

        def post(h, ym, yr, ys, p):
            h = _mixout(h, mix_n, ym, yr, ys, w_gate_b, b_gate3, w_branch_b, w_out_b, i)
            h = _ffn(h, ffb_n, ffb_gu, ffb_dn, i)
            return _ple(h, ple_n, p, ple_g_b, ple_p_b, final_g, i, final)

        hp, (mq, rq, rk, rv, rg, sq, mkt, mvt, skt, svt, mkb, mvb, skb, svb) = pre(hp, cos_p, sin_p, bp, tp, False)
        r3 = lambda a: a.reshape(bp, tp, a.shape[-1])
        ym = _moba_prompt(r3(mq), mkb, mvb)
        yr, st = _retention(r3(rq), r3(rk), r3(rv), r3(rg), s0_p, c_p)
        ys = _sb_prompt(r3(sq), skb, svb)
        hp = post(hp, ym.reshape(mp, bw), yr.reshape(mp, bw), ys.reshape(mp, bw), p_prompt[i].reshape(mp, -1))
        for lst, a in zip(new_p, (mkt, mvt, skt, svt, st)):
            lst.append(a)

        hs, (mq, rq, rk, rv, rg, sq, mkt, mvt, skt, svt, mk, mv, sk, sv) = pre(hs, cos_s, sin_s, ts, bs, True)
        seq_major = lambda a: a.reshape(ts, bs, a.shape[-1]).transpose(1, 0, 2)
        tok_major = lambda a: a.transpose(1, 0, 2).reshape(ms, a.shape[-1])
        new_rows = lambda a: jnp.pad(seq_major(a), ((0, 0), (0, NEW_ROWS - ts), (0, 0)))
        ym, ys = _sample_attend(page_table, _block_diag_queries(seq_major(mq)), _block_diag_queries(seq_major(sq)),
                                new_rows(mk), new_rows(mv), new_rows(sk), new_rows(sv), caches, i)
        yr, st = _retention(seq_major(rq), seq_major(rk), seq_major(rv), seq_major(rg), state_ret[i], c_s)
        hs = post(hs, bf(tok_major(ym)), tok_major(yr), bf(tok_major(ys)),
                  p_sample[i].transpose(1, 0, 2).reshape(ms, -1))
        for lst, a in zip(new_s, (mkt, mvt, skt, svt, st.astype(state_ret.dtype))):
            lst.append(a)

    y_prompt = hp.reshape(bp, tp, d)
    y_sample = hs.reshape(ts, bs, d).transpose(1, 0, 2)
    kv_p = [jnp.stack(l).reshape(depth, bp, H_MOBA, HEAD_DIM, tp).transpose(0, 1, 4, 2, 3) for l in new_p[:4]]
    kv_s = [jnp.stack(l).reshape(depth, ts, H_MOBA, HEAD_DIM, bs).transpose(0, 4, 1, 2, 3) for l in new_s[:4]]
    return (y_prompt, y_sample, *kv_p, jnp.stack(new_p[4]), *kv_s, jnp.stack(new_s[4]))
```

```python
import functools
import math

import jax
import jax.numpy as jnp
from jax import lax
from jax.experimental import pallas as pl
from jax.experimental.pallas import tpu as pltpu

D_MODEL = 1024
HEAD_DIM = 64
BRANCH_WIDTH = D_MODEL // 2
N_BRANCH = 3
H_MOBA = BRANCH_WIDTH // HEAD_DIM
MOBA_BLOCK = 256
MOBA_TOPK = 3
H_RET = 4
DV_RET = BRANCH_WIDTH // H_RET
DK_RET = DV_RET // 2
RET_CHUNK = 128
H_SB = BRANCH_WIDTH // HEAD_DIM
ROPE_BASE = 10000.0
RMS_EPS = 1e-6

LANES = 128
HEADS_PER_LANE_TILE = LANES // HEAD_DIM
VMEM_LIMIT_BYTES = 56 * 1024 * 1024

_IN_SIZES = (H_MOBA * HEAD_DIM,) * 3 + (H_RET * DK_RET,) * 2 + (H_RET * DV_RET,) * 2 + (H_SB * HEAD_DIM,) * 3
_IN_OFFS = tuple(int(sum(_IN_SIZES[:i])) for i in range(len(_IN_SIZES) + 1))
_KV_COLS = (1, 2, 8, 9)

_F32 = jnp.float32
_BF16 = jnp.bfloat16
_SCALE = HEAD_DIM ** -0.5
_NEG_INF = float("-inf")
SB_DEAD_CARRY = -110.0
MASK_BIAS = -1e30


def _params(*sem):
    return pltpu.CompilerParams(dimension_semantics=sem, vmem_limit_bytes=VMEM_LIMIT_BYTES)


def _resident(shape, index_map):
    return pl.BlockSpec(shape, index_map, pipeline_mode=pl.Buffered(1))


def _dot(a, b):
    return jnp.dot(a, b, preferred_element_type=_F32)


def _dot_nt(a, b):
    return lax.dot_general(a, b, (((1,), (1,)), ((), ())), preferred_element_type=_F32)


def _dot_tn(a, b):
    return lax.dot_general(a, b, (((0,), (0,)), ((), ())), preferred_element_type=_F32)


def _split2(x):
    hi = x.astype(_BF16)
    lo = (x - hi.astype(_F32)).astype(_BF16)
    return hi, lo


def _dot_precise(a, b):
    ah, al = _split2(a)
    bh, bl = _split2(b)
    return _dot(ah, bh) + (_dot(ah, bl) + _dot(al, bh))


def _dot_split_lhs(a, b_bf16):
    hi, lo = _split2(a)
    return _dot(hi, b_bf16) + _dot(lo, b_bf16)


def _rms(x, g):
    return x * lax.rsqrt(jnp.mean(x * x, axis=-1, keepdims=True) + RMS_EPS) * g


def _ffn_body(x_ref, g_ref, wgu_ref, wd_ref, o_ref, *, d_ff, fc):
    x = x_ref[...]
    xb = _rms(x, g_ref[...]).astype(_BF16)
    acc = None
    for c in range(d_ff // fc):
        g = _dot(xb, wgu_ref[:, c * fc:(c + 1) * fc])
        u = _dot(xb, wgu_ref[:, d_ff + c * fc:d_ff + (c + 1) * fc])
        a = (g * jax.nn.sigmoid(g) * u).astype(_BF16)
        part = _dot(a, wd_ref[c * fc:(c + 1) * fc, :])
        acc = part if acc is None else acc + part
    o_ref[...] = x + 0.5 * acc


def _row_tile(m):
    return min(m, 512)


def _ffn(x, g, wgu, wd, layer):
    m, d = x.shape
    d_ff = wd.shape[1]
    tm = _row_tile(m)
    return pl.pallas_call(
        functools.partial(_ffn_body, d_ff=d_ff, fc=256),
        grid=(m // tm,),
        in_specs=[
            pl.BlockSpec((tm, d), lambda i: (i, 0)),
            _resident((None, 1, d), lambda i: (layer, 0, 0)),
            _resident((None, d, 2 * d_ff), lambda i: (layer, 0, 0)),
            _resident((None, d_ff, d), lambda i: (layer, 0, 0)),
        ],
        out_specs=pl.BlockSpec((tm, d), lambda i: (i, 0)),
        out_shape=jax.ShapeDtypeStruct((m, d), _F32),
        compiler_params=_params("parallel"),
        name="ffn",
    )(x, g, wgu, wd)


def _rotary_apply(x, c, s):
    w = x.shape[1]
    half = DK_RET // 2
    lane = lax.broadcasted_iota(jnp.int32, x.shape, 1)
    up = pltpu.roll(x, w - half, 1)
    dn = pltpu.roll(x, half, 1)
    partner = jnp.where((lane % DK_RET) < half, up, dn)
    return x * c + partner * s


def _inproj_body(h_ref, g_ref, w_ref, wt_ref, cos_ref, sin_ref, *out_refs, token_major_kv):
    ub = _rms(h_ref[...], g_ref[...]).astype(_BF16)

    def proj(n):
        return _dot(ub, w_ref[:, _IN_OFFS[n]:_IN_OFFS[n + 1]])

    mq_ref, rq_ref, rk_ref, rv_ref, rg_ref, sq_ref = out_refs[:6]
    kvt_refs = out_refs[6:10]
    rest = out_refs[10:]
    mq_ref[...] = proj(0)
    c = cos_ref[...]
    s = sin_ref[...]
    rq_ref[...] = _rotary_apply(proj(3), c, s)
    rk_ref[...] = _rotary_apply(proj(4), c, s) * (DK_RET ** -0.5)
    rv_ref[...] = proj(5)
    rg_ref[...] = proj(6)
    sq_ref[...] = proj(7)
    bw = BRANCH_WIDTH
    for n in range(4):
        t = _dot_nt(wt_ref[n * bw:(n + 1) * bw, :], ub)
        groups, _, lw = kvt_refs[n].shape
        for gi in range(groups):
            kvt_refs[n][gi] = t[:, gi * lw:(gi + 1) * lw]
        if token_major_kv:
            rest[n][...] = proj(_KV_COLS[n])
        else:
            for cb in range(rest[n].shape[1]):
                rest[n][0, cb] = t[:, cb * MOBA_BLOCK:(cb + 1) * MOBA_BLOCK].astype(_BF16)


def _inproj(h, g, w_in, w_kvt, cos_t, sin_t, layer, groups, group_len, token_major_kv):
    m, d = h.shape
    tm = _row_tile(m)
    pos_tiles = cos_t.shape[0] // tm
    bw = BRANCH_WIDTH
    rw = H_RET * DK_RET
    row = lambda w: pl.BlockSpec((tm, w), lambda i: (i, 0))
    widths = (bw, rw, rw, bw, bw, bw)
    out_shape = [jax.ShapeDtypeStruct((m, w), _F32) for w in widths]
    out_specs = [row(w) for w in widths]
    if group_len >= tm:
        tiles_per_group = group_len // tm
        kvt_spec = pl.BlockSpec((1, bw, tm), lambda i: (i // tiles_per_group, 0, i % tiles_per_group))
    else:
        assert m == tm, "short groups need all rows in one tile"
        kvt_spec = pl.BlockSpec((groups, bw, group_len), lambda i: (0, 0, 0))
    out_shape += [jax.ShapeDtypeStruct((groups, bw, group_len), _F32)] * 4
    out_specs += [kvt_spec] * 4
    if token_major_kv:
        out_shape += [jax.ShapeDtypeStruct((m, bw), _F32)] * 4
        out_specs += [row(bw)] * 4
    else:
        nb = group_len // MOBA_BLOCK
        tb = tm // MOBA_BLOCK
        out_shape += [jax.ShapeDtypeStruct((groups, nb, bw, MOBA_BLOCK), _BF16)] * 4
        out_specs += [pl.BlockSpec((1, tb, bw, MOBA_BLOCK),
                                   lambda i: (i // tiles_per_group, i % tiles_per_group, 0, 0))] * 4
    return pl.pallas_call(
        functools.partial(_inproj_body, token_major_kv=token_major_kv),
        grid=(m // tm,),
        in_specs=[
            row(d),
            _resident((None, 1, d), lambda i: (layer, 0, 0)),
            _resident((None, d, w_in.shape[2]), lambda i: (layer, 0, 0)),
            _resident((None, 4 * bw, d), lambda i: (layer, 0, 0)),
            pl.BlockSpec((tm, rw), lambda i: (i % pos_tiles, 0)),
            pl.BlockSpec((tm, rw), lambda i: (i % pos_tiles, 0)),
        ],
        out_specs=out_specs,
        out_shape=out_shape,
        compiler_params=_params("parallel"),
        name="inproj",
    )(h, g, w_in, w_kvt, cos_t, sin_t)


def _mixout_body(h_ref, g_ref, ym_ref, yr_ref, ys_ref, wg_ref, bg_ref, wb_ref, wo_ref, o_ref):
    h = h_ref[...]
    d = h.shape[1]
    ub = _rms(h, g_ref[...]).astype(_BF16)
    merged = None
    for n, y_ref in enumerate((ym_ref, yr_ref, ys_ref)):
        gate = jax.nn.sigmoid(_dot(ub, wg_ref[:, n * d:(n + 1) * d]) + bg_ref[:, n * d:(n + 1) * d])
        term = gate * _dot(y_ref[...], wb_ref[n])
        merged = term if merged is None else merged + term
    o_ref[...] = h + _dot(merged.astype(_BF16), wo_ref[...])


def _mixout(h, g, ym, yr, ys, w_gate, b_gate, w_branch, w_out, layer):
    m, d = h.shape
    tm = _row_tile(m)
    bw = BRANCH_WIDTH
    row = lambda w: pl.BlockSpec((tm, w), lambda i: (i, 0))
    return pl.pallas_call(
        _mixout_body,
        grid=(m // tm,),
        in_specs=[
            row(d),
            _resident((None, 1, d), lambda i: (layer, 0, 0)),
            row(bw), row(bw), row(bw),
            _resident((None, d, N_BRANCH * d), lambda i: (layer, 0, 0)),
            _resident((None, 1, N_BRANCH * d), lambda i: (layer, 0, 0)),
            _resident((None, N_BRANCH, bw, d), lambda i: (layer, 0, 0, 0)),
            _resident((None, d, d), lambda i: (layer, 0, 0)),
        ],
        out_specs=row(d),
        out_shape=jax.ShapeDtypeStruct((m, d), _F32),
        compiler_params=_params("parallel"),
        name="mixout",
    )(h, g, ym, yr, ys, w_gate, b_gate, w_branch, w_out)


def _ple_body(h_ref, g_ref, p_ref, wg_ref, wp_ref, fg_ref, o_ref, *, final):
    h = h_ref[...]
    ub = _rms(h, g_ref[...]).astype(_BF16)
    gate = jax.nn.sigmoid(_dot(ub, wg_ref[...]))
    out = h + gate * _dot(p_ref[...].astype(_BF16), wp_ref[...])
    if final:
        out = _rms(out, fg_ref[...])
    o_ref[...] = out


def _ple(h, g, p, w_gate, w_proj, final_g, layer, final):
    m, d = h.shape
    tm = _row_tile(m)
    dp = p.shape[1]
    row = lambda w: pl.BlockSpec((tm, w), lambda i: (i, 0))
    return pl.pallas_call(
        functools.partial(_ple_body, final=final),
        grid=(m // tm,),
        in_specs=[
            row(d),
            _resident((None, 1, d), lambda i: (layer, 0, 0)),
            row(dp),
            _resident((None, d, d), lambda i: (layer, 0, 0)),
            _resident((None, dp, d), lambda i: (layer, 0, 0)),
            _resident((1, d), lambda i: (0, 0)),
        ],
        out_specs=row(d),
        out_shape=jax.ShapeDtypeStruct((m, d), _F32),
        compiler_params=_params("parallel"),
        name="ple",
    )(h, g, p, w_gate, w_proj, final_g)


def _top_blocks(gate, blkf, own):
    valid = blkf < own
    g = jnp.where(valid, gate, jnp.finfo(_F32).min)
    sel = jnp.zeros(gate.shape, _F32)
    for _ in range(MOBA_TOPK):
        m = jnp.max(g, axis=1, keepdims=True)
        first = jnp.min(jnp.where(g == m, blkf, float(LANES)), axis=1, keepdims=True)
        pick = blkf == first
        sel = jnp.where(jnp.logical_and(pick, valid), 1.0, sel)
        g = jnp.where(pick, _NEG_INF, g)
    return sel


def _head_mask(hh):
    lane = lax.broadcasted_iota(jnp.int32, (1, LANES), 1)
    return (lane // HEAD_DIM) == hh


def _upper_tri(n):
    r = lax.broadcasted_iota(jnp.int32, (n, n), 0)
    c = lax.broadcasted_iota(jnp.int32, (n, n), 1)
    return jnp.where(r > c, 1.0, 0.0).astype(_BF16)


def _stick_weights(z, u_tri, carry, strict):
    sp = jnp.log(1.0 + jnp.exp(-jnp.abs(z)))
    log_keep = -(jnp.maximum(z, 0.0) + sp)
    log_beta = jnp.minimum(z, 0.0) - sp
    if strict is not None:
        log_keep = jnp.where(strict, log_keep, 0.0)
    later = _dot_split_lhs(log_keep, u_tri) + carry
    a = jnp.exp(log_beta + later)
    if strict is not None:
        a = jnp.where(strict, a, 0.0)
    return a, carry + jnp.sum(log_keep, axis=1, keepdims=True)


def _moba_prompt_body(q_ref, k_ref, v_ref, o_ref, kmt):
    i = pl.program_id(2)
    blk = MOBA_BLOCK
    nb = k_ref.shape[1]

    @pl.when(i == 0)
    def _block_means():
        mean_w = jnp.full((blk, LANES), 1.0 / blk, _BF16)
        lane = lax.broadcasted_iota(jnp.int32, (LANES, LANES), 1)
        acc = jnp.zeros((LANES, LANES), _F32)
        for j in range(nb):
            acc = jnp.where(lane == j, _dot(k_ref[0, j], mean_w), acc)
        kmt[...] = acc

    q_raw = q_ref[0]
    row = lax.broadcasted_iota(jnp.int32, (blk, blk), 0)
    col = lax.broadcasted_iota(jnp.int32, (blk, blk), 1)
    causal = col <= row
    blkf = lax.broadcasted_iota(jnp.int32, (blk, LANES), 1).astype(_F32)
    own_f = i.astype(_F32)
    heads = range(HEADS_PER_LANE_TILE)
    qbs, state = [], []
    kt = k_ref[0, i]
    vt = v_ref[0, i]
    for hh in heads:
        qh = jnp.where(_head_mask(hh), q_raw, 0.0)
        sel = _top_blocks(_dot_precise(qh, kmt[...]), blkf, own_f)
        qb = (qh * _SCALE).astype(_BF16)
        qbs.append(jnp.concatenate([qb, jnp.where(sel > 0.0, 0.0, MASK_BIAS).astype(_BF16)], axis=1))
        s = jnp.where(causal, _dot(qb, kt), _NEG_INF)
        m = jnp.max(s, axis=1, keepdims=True)
        p = jnp.exp(s - m)
        state += [m, jnp.sum(p, axis=1, keepdims=True), _dot_nt(p.astype(_BF16), vt)]
    brow = lax.broadcasted_iota(jnp.int32, (LANES, blk), 0)

    def body(j, st):
        pick_row = jnp.where(brow == j, 1.0, 0.0).astype(_BF16)
        kt = jnp.concatenate([k_ref[0, j], pick_row], axis=0)
        vt = v_ref[0, j]
        out = []
        for hh in heads:
            m, l, acc = st[3 * hh:3 * hh + 3]
            s = _dot(qbs[hh], kt)
            m_new = jnp.maximum(m, jnp.max(s, axis=1, keepdims=True))
            alpha = jnp.exp(m - m_new)
            p = jnp.exp(s - m_new)
            l = alpha * l + jnp.sum(p, axis=1, keepdims=True)
            acc = alpha * acc + _dot_nt(p.astype(_BF16), vt)
            out += [m_new, l, acc]
        return tuple(out)

    st = lax.fori_loop(0, i, body, tuple(state))
    o_ref[0] = jnp.where(_head_mask(0), st[2] / st[1], st[5] / st[4]).astype(o_ref.dtype)


def _kv_block_spec(nb):
    return pl.BlockSpec((1, nb, LANES, MOBA_BLOCK), lambda bi, hp, i: (bi, 0, hp, 0))


def _moba_prompt(q, kt, vt):
    b, t, w = q.shape
    blk = MOBA_BLOCK
    nb = t // blk
    assert nb <= LANES
    return pl.pallas_call(
        _moba_prompt_body,
        grid=(b, w // LANES, nb),
        in_specs=[
            pl.BlockSpec((1, blk, LANES), lambda bi, hp, i: (bi, i, hp)),
            _kv_block_spec(nb),
            _kv_block_spec(nb),
        ],
        out_specs=pl.BlockSpec((1, blk, LANES), lambda bi, hp, i: (bi, i, hp)),
        out_shape=jax.ShapeDtypeStruct((b, t, w), _BF16),
        scratch_shapes=[pltpu.VMEM((LANES, LANES), _F32)],
        compiler_params=_params("parallel", "parallel", "arbitrary"),
        name="moba_prompt",
    )(q, kt, vt)


def _sb_prompt_body(q_ref, k_ref, v_ref, o_ref):
    i = pl.program_id(2)
    tq = MOBA_BLOCK
    q_raw = q_ref[0]
    row = lax.broadcasted_iota(jnp.int32, (tq, tq), 0)
    col = lax.broadcasted_iota(jnp.int32, (tq, tq), 1)
    strict = col < row
    u_tri = _upper_tri(tq)
    heads = range(HEADS_PER_LANE_TILE)
    qbs = [(jnp.where(_head_mask(hh), q_raw, 0.0) * _SCALE).astype(_BF16) for hh in heads]

    def visit(j, carries, accs, mask):
        kt = k_ref[0, j]
        vt = v_ref[0, j]
        new_c, new_a = [], []
        for hh in heads:
            a, c = _stick_weights(_dot(qbs[hh], kt), u_tri, carries[hh], mask)
            new_c.append(c)
            new_a.append(accs[hh] + _dot_nt(a.astype(_BF16), vt))
        return new_c, new_a

    def live_of(carries):
        return jnp.maximum(jnp.max(carries[0]), jnp.max(carries[1]))

    zero_c = jnp.zeros((tq, 1), _F32)
    zero_a = jnp.zeros((tq, LANES), _F32)
    carries, accs = visit(i, [zero_c, zero_c], [zero_a, zero_a], strict)

    def cond(st):
        return jnp.logical_and(st[0] < i, st[5] > SB_DEAD_CARRY)

    def body(st):
        n, c0, c1, a0, a1, _ = st
        (c0, c1), (a0, a1) = visit(i - 1 - n, [c0, c1], [a0, a1], None)
        return n + 1, c0, c1, a0, a1, live_of([c0, c1])

    st = lax.while_loop(cond, body, (jnp.int32(0), carries[0], carries[1], accs[0], accs[1], live_of(carries)))
    o_ref[0] = jnp.where(_head_mask(0), st[3], st[4]).astype(o_ref.dtype)


def _sb_prompt(q, kt, vt):
    b, t, w = q.shape
    tq = MOBA_BLOCK
    nb = t // tq
    return pl.pallas_call(
        _sb_prompt_body,
        grid=(b, w // LANES, nb),
        in_specs=[
            pl.BlockSpec((1, tq, LANES), lambda bi, hp, i: (bi, i, hp)),
            _kv_block_spec(nb),
            _kv_block_spec(nb),
        ],
        out_specs=pl.BlockSpec((1, tq, LANES), lambda bi, hp, i: (bi, i, hp)),
        out_shape=jax.ShapeDtypeStruct((b, t, w), _BF16),
        compiler_params=_params("parallel", "parallel", "arbitrary"),
        name="sb_prompt",
    )(q, kt, vt)


def _ret_body(rq_ref, rk_ref, rv_ref, rg_ref, s0_ref, dm_ref, qd_ref, kd_ref, cd_ref, y_ref, sf_ref, state):
    n = pl.program_id(1)

    @pl.when(n == 0)
    def _init():
        for h in range(H_RET):
            off = DK_RET * (h % HEADS_PER_LANE_TILE)
            state[h] = jnp.zeros((LANES, DV_RET), _F32)
            state[h, off:off + DK_RET, :] = s0_ref[0, h]

    for pair in range(H_RET // HEADS_PER_LANE_TILE):
        qp = rq_ref[0, :, pair * LANES:(pair + 1) * LANES]
        kp = rk_ref[0, :, pair * LANES:(pair + 1) * LANES]
        kpb = kp.astype(_BF16)
        for hh in range(HEADS_PER_LANE_TILE):
            h = pair * HEADS_PER_LANE_TILE + hh
            hm = _head_mask(hh)
            qm = jnp.where(hm, qp, 0.0).astype(_BF16)
            inner = _dot_nt(qm, kpb) * dm_ref[h]
            vb = rv_ref[0, :, h * DV_RET:(h + 1) * DV_RET].astype(_BF16)
            s = state[h]
            o = _dot(inner.astype(_BF16), vb) + _dot(qm, s.astype(_BF16)) * qd_ref[h]
            kdec = (jnp.where(hm, kp, 0.0) * kd_ref[h]).astype(_BF16)
            state[h] = s * cd_ref[h] + _dot_tn(kdec, vb)
            on = o * lax.rsqrt(jnp.mean(o * o, axis=-1, keepdims=True) + RMS_EPS)
            g = rg_ref[0, :, h * DV_RET:(h + 1) * DV_RET]
            y_ref[0, :, h * DV_RET:(h + 1) * DV_RET] = (on * (g * jax.nn.sigmoid(g))).astype(y_ref.dtype)

    @pl.when(n == pl.num_programs(1) - 1)
    def _fin():
        for h in range(H_RET):
            off = DK_RET * (h % HEADS_PER_LANE_TILE)
            sf_ref[0, h] = state[h, off:off + DK_RET, :]


def _ret_consts(c):
    log_g = jnp.log(1.0 - 2.0 ** (-5.0 - jnp.arange(H_RET, dtype=_F32)))
    i = jnp.arange(c, dtype=_F32)
    diff = i[:, None] - i[None, :]
    dm = jnp.where(diff >= 0, jnp.exp(log_g[:, None, None] * jnp.maximum(diff, 0.0)), 0.0)
    qd = jnp.broadcast_to(jnp.exp(log_g[:, None] * (i[None, :] + 1.0))[:, :, None], (H_RET, c, DV_RET))
    kd = jnp.broadcast_to(jnp.exp(log_g[:, None] * (c - 1.0 - i[None, :]))[:, :, None], (H_RET, c, LANES))
    cd = jnp.broadcast_to(jnp.exp(log_g * c)[:, None, None], (H_RET, 1, DV_RET))
    return dm, qd, kd, cd


def _retention(rq, rk, rv, rg, s0, c):
    b, t, _ = rq.shape
    n = t // c
    dm, qd, kd, cd = _ret_consts(c)
    rw = H_RET * DK_RET
    bw = BRANCH_WIDTH
    const = lambda a: _resident(a.shape, lambda bi, ni: (0,) * a.ndim)
    return pl.pallas_call(
        _ret_body,
        grid=(b, n),
        in_specs=[
            pl.BlockSpec((1, c, rw), lambda bi, ni: (bi, ni, 0)),
            pl.BlockSpec((1, c, rw), lambda bi, ni: (bi, ni, 0)),
            pl.BlockSpec((1, c, bw), lambda bi, ni: (bi, ni, 0)),
            pl.BlockSpec((1, c, bw), lambda bi, ni: (bi, ni, 0)),
            pl.BlockSpec((1, H_RET, DK_RET, DV_RET), lambda bi, ni: (bi, 0, 0, 0)),
            const(dm), const(qd), const(kd), const(cd),
        ],
        out_specs=[
            pl.BlockSpec((1, c, bw), lambda bi, ni: (bi, ni, 0)),
            pl.BlockSpec((1, H_RET, DK_RET, DV_RET), lambda bi, ni: (bi, 0, 0, 0)),
        ],
        out_shape=[
            jax.ShapeDtypeStruct((b, t, bw), _BF16),
            jax.ShapeDtypeStruct((b, H_RET, DK_RET, DV_RET), _F32),
        ],
        scratch_shapes=[pltpu.VMEM((H_RET, LANES, DV_RET), _F32)],
        compiler_params=_params("parallel", "arbitrary"),
        name="retention",
    )(rq, rk, rv, rg, s0, dm, qd, kd, cd)


NEW_ROWS = 16
SAMPLE_BLOCKS_PER_STEP = 4


def _gather_heads(o, n_tok):
    r, w = o.shape
    hrow = lax.broadcasted_iota(jnp.int32, (r, w), 0) % H_MOBA
    hlane = lax.broadcasted_iota(jnp.int32, (r, w), 1) // HEAD_DIM
    om = jnp.where(hrow == hlane, o, 0.0)
    return jnp.sum(om.reshape(n_tok, H_MOBA, w), axis=1)


def _page_pair_t(p0_ref, p1_ref):
    w = p0_ref.shape[0] * p0_ref.shape[1]
    pg = p0_ref.shape[2]
    return jnp.concatenate([p0_ref[...].reshape(w, pg), p1_ref[...].reshape(w, pg)], axis=1).astype(_BF16)


def _new_rows(ref):
    x = ref[0]
    return jnp.concatenate([x, jnp.zeros((LANES - x.shape[0], x.shape[1]), _F32)], axis=0).astype(_BF16)


def _sample_body(pt_ref, qm_ref, qs_ref, mkn_ref, mvn_ref, skn_ref, svn_ref, *refs, n_blk, n_tok, bps):
    del pt_ref
    npg = 2 * bps
    mk_p, mv_p, sk_p, sv_p = (refs[c * npg:(c + 1) * npg] for c in range(4))
    ym_ref, ys_ref, kmt, m_out, m_max, m_sum, s_carry, s_acc = refs[4 * npg:]
    j = pl.program_id(1)
    n_steps = n_blk // bps
    base = (n_steps - 1 - j) * bps
    blk = MOBA_BLOCK
    r = qm_ref.shape[1]
    qmb = (qm_ref[0] * _SCALE).astype(_BF16)
    qsb = (qs_ref[0] * _SCALE).astype(_BF16)
    tok = lax.broadcasted_iota(jnp.int32, (r, LANES), 0) // H_MOBA
    colo = lax.broadcasted_iota(jnp.int32, (r, LANES), 1)

    @pl.when(j == 0)
    def _own_sb():
        a, carry = _stick_weights(_dot_nt(qsb, _new_rows(skn_ref)), _upper_tri(LANES),
                                  jnp.zeros((r, 1), _F32), colo < tok)
        s_carry[...] = jnp.broadcast_to(carry, s_carry.shape)
        s_acc[...] = _dot(a.astype(_BF16), _new_rows(svn_ref))
        kmt[...] = jnp.zeros(kmt.shape, _F32)

    u_tri = _upper_tri(blk)
    lane = lax.broadcasted_iota(jnp.int32, (blk, LANES), 1)
    carry = s_carry[:, 0:1]
    acc_s = s_acc[...]
    kmt_add = None
    for b in range(bps - 1, -1, -1):
        jb = base + b
        a, carry = _stick_weights(_dot(qsb, _page_pair_t(sk_p[2 * b], sk_p[2 * b + 1])), u_tri, carry, None)
        acc_s = acc_s + _dot_nt(a.astype(_BF16), _page_pair_t(sv_p[2 * b], sv_p[2 * b + 1]))
        kt = _page_pair_t(mk_p[2 * b], mk_p[2 * b + 1])
        s = _dot(qmb, kt)
        m = jnp.max(s, axis=1, keepdims=True)
        p = jnp.exp(s - m)
        m_out[jb] = _dot_nt(p.astype(_BF16), _page_pair_t(mv_p[2 * b], mv_p[2 * b + 1]))
        m_max[jb] = jnp.broadcast_to(m, (r, LANES))
        m_sum[jb] = jnp.broadcast_to(jnp.sum(p, axis=1, keepdims=True), (r, LANES))
        part = _dot(kt, jnp.where(lane == jb, 1.0 / blk, 0.0).astype(_BF16))
        kmt_add = part if kmt_add is None else kmt_add + part
    s_carry[...] = jnp.broadcast_to(carry, s_carry.shape)
    s_acc[...] = acc_s
    kmt[...] += kmt_add

    @pl.when(j == n_steps - 1)
    def _finish():
        ys_ref[0] = _gather_heads(s_acc[...], n_tok).astype(ys_ref.dtype)
        blkf = colo.astype(_F32)
        sel = _top_blocks(_dot_precise(qm_ref[0], kmt[...]), blkf, float(n_blk))
        s_own = jnp.where(colo <= tok, _dot_nt(qmb, _new_rows(mkn_ref)), _NEG_INF)
        m_own = jnp.max(s_own, axis=1, keepdims=True)
        picked = [sel[:, b:b + 1] > 0.0 for b in range(n_blk)]
        m_all = m_own
        for b in range(n_blk):
            m_all = jnp.maximum(m_all, jnp.where(picked[b], m_max[b][:, 0:1], _NEG_INF))
        p_own = jnp.exp(s_own - m_all)
        l = jnp.sum(p_own, axis=1, keepdims=True)
        acc = _dot(p_own.astype(_BF16), _new_rows(mvn_ref))
        for b in range(n_blk):
            wgt = jnp.where(picked[b], jnp.exp(m_max[b][:, 0:1] - m_all), 0.0)
            l = l + wgt * m_sum[b][:, 0:1]
            acc = acc + wgt * m_out[b]
        ym_ref[0] = _gather_heads(acc / l, n_tok).astype(ym_ref.dtype)


def _sample_attend(page_table, qm, qs, mkn, mvn, skn, svn, caches, layer):
    b, r, w = qm.shape
    n_tok = r // H_MOBA
    _, _, nh, hd, pg = caches[0].shape
    n_blk = page_table.shape[1] * pg // MOBA_BLOCK
    bps = math.gcd(n_blk, SAMPLE_BLOCKS_PER_STEP)
    n_steps = n_blk // bps
    npg = bps * MOBA_BLOCK // pg
    assert npg == 2 * bps and n_blk <= LANES
    page = lambda k: pl.BlockSpec(
        (None, None, nh, hd, pg), lambda bi, j, pt: (layer, pt[bi, npg * (n_steps - 1 - j) + k], 0, 0, 0))
    seq = lambda rows: pl.BlockSpec((1, rows, w), lambda bi, j, pt: (bi, 0, 0))
    cache_specs = [page(k) for _ in range(4) for k in range(npg)]
    cache_args = [c for c in caches for _ in range(npg)]
    out_spec = pl.BlockSpec((1, n_tok, w), lambda bi, j, pt: (bi, 0, 0))
    return pl.pallas_call(
        functools.partial(_sample_body, n_blk=n_blk, n_tok=n_tok, bps=bps),
        grid_spec=pltpu.PrefetchScalarGridSpec(
            num_scalar_prefetch=1,
            grid=(b, n_steps),
            in_specs=[seq(r), seq(r)] + [seq(NEW_ROWS)] * 4 + cache_specs,
            out_specs=[out_spec, out_spec],
            scratch_shapes=[
                pltpu.VMEM((w, LANES), _F32),
                pltpu.VMEM((n_blk, r, w), _F32),
                pltpu.VMEM((n_blk, r, LANES), _F32),
                pltpu.VMEM((n_blk, r, LANES), _F32),
                pltpu.VMEM((r, LANES), _F32),
                pltpu.VMEM((r, w), _F32),
            ],
        ),
        out_shape=[jax.ShapeDtypeStruct((b, n_tok, w), _F32)] * 2,
        compiler_params=_params("parallel", "arbitrary"),
        name="sample_attend",
    )(page_table, qm, qs, mkn, mvn, skn, svn, *cache_args)


def _rotary_tables(pos, reps):
    inv = ROPE_BASE ** (-jnp.arange(0, DK_RET, 2, dtype=_F32) / DK_RET)
    ang = pos.astype(_F32)[:, None] * inv[None, :]
    cos = jnp.cos(ang)
    sin = jnp.sin(ang)
    c = jnp.tile(jnp.concatenate([cos, cos], axis=1), (reps, H_RET))
    s = jnp.tile(jnp.concatenate([-sin, sin], axis=1), (reps, H_RET))
    return c, s


def _block_diag_queries(q):
    b, n_tok, w = q.shape
    head = jnp.arange(w, dtype=jnp.int32) // HEAD_DIM
    keep = head[None, :] == jnp.arange(H_MOBA, dtype=jnp.int32)[:, None]
    return jnp.where(keep[None, None], q[:, :, None, :], 0.0).reshape(b, n_tok * H_MOBA, w)


def kernel(x_prompt, x_sample, cache_moba_k, cache_moba_v, cache_sb_k, cache_sb_v, state_ret, page_table,
           p_prompt, p_sample, ffa_norm, ffa_w_gu, ffa_w_down, mix_norm, w_in, w_gate, b_gate, w_branch,
           w_out, ffb_norm, ffb_w_gu, ffb_w_down, ple_norm, ple_w_gate, ple_w_proj, final_norm):
    bp, tp, d = x_prompt.shape
    bs, ts, _ = x_sample.shape
    depth = w_in.shape[0]
    pg = cache_moba_k.shape[2]
    past_len = page_table.shape[1] * pg
    bw = BRANCH_WIDTH

    bf = lambda a: a.astype(_BF16)
    vec = lambda a: a.reshape(depth, 1, a.shape[-1])
    ffa_gu, ffa_dn, ffb_gu, ffb_dn = bf(ffa_w_gu), bf(ffa_w_down), bf(ffb_w_gu), bf(ffb_w_down)
    w_in_b, w_gate_b, w_branch_b, w_out_b = bf(w_in), bf(w_gate), bf(w_branch), bf(w_out)
    w_kvt = bf(jnp.concatenate([w_in[:, :, _IN_OFFS[n]:_IN_OFFS[n + 1]] for n in _KV_COLS], axis=2)
               .transpose(0, 2, 1))
    ple_g_b, ple_p_b = bf(ple_w_gate), bf(ple_w_proj)
    ffa_n, mix_n, ffb_n, ple_n, b_gate3 = vec(ffa_norm), vec(mix_norm), vec(ffb_norm), vec(ple_norm), vec(b_gate)
    final_g = final_norm.reshape(1, d)

    caches = [c.transpose(0, 1, 3, 4, 2) for c in (cache_moba_k, cache_moba_v, cache_sb_k, cache_sb_v)]

    mp, ms = bp * tp, bs * ts
    cos_p, sin_p = _rotary_tables(jnp.arange(tp, dtype=jnp.int32), 1)
    pos_s = jnp.repeat(past_len + jnp.arange(ts, dtype=jnp.int32), bs)
    cos_s, sin_s = _rotary_tables(pos_s, 1)

    hp = x_prompt.reshape(mp, d)
    hs = x_sample.transpose(1, 0, 2).reshape(ms, d)
    s0_p = jnp.zeros((bp, H_RET, DK_RET, DV_RET), _F32)
    c_p = math.gcd(tp, RET_CHUNK)
    c_s = math.gcd(ts, RET_CHUNK)
    new_p = [[] for _ in range(5)]
    new_s = [[] for _ in range(5)]

    for i in range(depth):
        final = i == depth - 1

        def pre(h, cos_t, sin_t, groups, group_len, token_major_kv):
            h = _ffn(h, ffa_n, ffa_gu, ffa_dn, i)
            return h, _inproj(h, mix_n, w_in_b, w_kvt, cos_t, sin_t, i, groups, group_len, token_major_kv)

        def post(h, ym, yr, ys, p):
            h = _mixout(h, mix_n, ym, yr, ys, w_gate_b, b_gate3, w_branch_b, w_out_b, i)
            h = _ffn(h, ffb_n, ffb_gu, ffb_dn, i)
            return _ple(h, ple_n, p, ple_g_b, ple_p_b, final_g, i, final)

        hp, (mq, rq, rk, rv, rg, sq, mkt, mvt, skt, svt, mkb, mvb, skb, svb) = pre(hp, cos_p, sin_p, bp, tp, False)
        r3 = lambda a: a.reshape(bp, tp, a.shape[-1])
        ym = _moba_prompt(r3(mq), mkb, mvb)
        yr, st = _retention(r3(rq), r3(rk), r3(rv), r3(rg), s0_p, c_p)
        ys = _sb_prompt(r3(sq), skb, svb)
        hp = post(hp, ym.reshape(mp, bw), yr.reshape(mp, bw), ys.reshape(mp, bw), p_prompt[i].reshape(mp, -1))
        for lst, a in zip(new_p, (mkt, mvt, skt, svt, st)):
            lst.append(a)

        hs, (mq, rq, rk, rv, rg, sq, mkt, mvt, skt, svt, mk, mv, sk, sv) = pre(hs, cos_s, sin_s, ts, bs, True)
        seq_major = lambda a: a.reshape(ts, bs, a.shape[-1]).transpose(1, 0, 2)
        tok_major = lambda a: a.transpose(1, 0, 2).reshape(ms, a.shape[-1])
        new_rows = lambda a: jnp.pad(seq_major(a), ((0, 0), (0, NEW_ROWS - ts), (0, 0)))
        ym, ys = _sample_attend(page_table, _block_diag_queries(seq_major(mq)), _block_diag_queries(seq_major(sq)),
                                new_rows(mk), new_rows(mv), new_rows(sk), new_rows(sv), caches, i)
        yr, st = _retention(seq_major(rq), seq_major(rk), seq_major(rv), seq_major(rg), state_ret[i], c_s)
        hs = post(hs, bf(tok_major(ym)), tok_major(yr), bf(tok_major(ys)),
                  p_sample[i].transpose(1, 0, 2).reshape(ms, -1))
        for lst, a in zip(new_s, (mkt, mvt, skt, svt, st.astype(state_ret.dtype))):
            lst.append(a)

    y_prompt = hp.reshape(bp, tp, d)
    y_sample = hs.reshape(ts, bs, d).transpose(1, 0, 2)
    kv_p = [jnp.stack(l).reshape(depth, bp, H_MOBA, HEAD_DIM, tp).transpose(0, 1, 4, 2, 3) for l in new_p[:4]]
    kv_s = [jnp.stack(l).reshape(depth, ts, H_MOBA, HEAD_DIM, bs).transpose(0, 4, 1, 2, 3) for l in new_s[:4]]
    return (y_prompt, y_sample, *kv_p, jnp.stack(new_p[4]), *kv_s, jnp.stack(new_s[4]))
```
